```python
import math
import jax, jax.numpy as jnp
from jax import lax
import numpy as np

D_MODEL = 1024
BATCH = 16
SEQ = 4096
DEPTH = 4

N_A = DEPTH // 2
N_B = DEPTH - N_A
HEAD_DIM = 64
N_MAIN_HEADS = 12
D_MAIN = N_MAIN_HEADS * HEAD_DIM
N_MEM_HEADS = 4
D_MEMH = N_MEM_HEADS * HEAD_DIM
D_MIX = D_MAIN + D_MEMH
MEM_LEN = 256
D_DECAY_LORA = 64
D_AAA_LORA = 64
D_GATE_LORA = 160
D_SHIFT = 3 * D_MAIN + D_DECAY_LORA + D_AAA_LORA + D_GATE_LORA
D_IN_A = D_SHIFT + D_MEMH
D_IN_B = 2 * D_MAIN + D_MEMH
D_KV = 2 * D_MAIN + N_MAIN_HEADS
D_FF = 256 * (-(-8 * D_MODEL // (3 * 256)))
BLOCK_Q = 128
RMS_EPS = 1e-6
GN_EPS = 64e-5
ATTN_SCALE = HEAD_DIM ** -0.5
NEG_INF = -1e30

kernel_name = "rwkv7_fox_yoco_memory_hybrid"


def rms_norm(x, g, eps=RMS_EPS):
    xf = x.astype(jnp.float32)
    y = xf * lax.rsqrt(jnp.mean(xf * xf, axis=-1, keepdims=True) + eps)
    return (y * g.astype(jnp.float32)).astype(x.dtype)


def token_shift(u):
    return jnp.pad(u[:, :-1], ((0, 0), (1, 0), (0, 0)))


def split_heads(u, n):
    return u.reshape(u.shape[:-1] + (n, HEAD_DIM))


def swiglu(h, w_gate_up, w_down):
    gu = h @ w_gate_up
    return (jax.nn.silu(gu[..., :D_FF]) * gu[..., D_FF:]) @ w_down


def memory_attention(q_cols, mem, mem_norm_g, w_mem_kv, q_gain, k_gain):
    bsz, t = q_cols.shape[:2]
    q = rms_norm(split_heads(q_cols, N_MEM_HEADS), q_gain)
    kv = rms_norm(mem, mem_norm_g) @ w_mem_kv
    k = rms_norm(split_heads(kv[..., :D_MEMH], N_MEM_HEADS), k_gain)
    v = split_heads(kv[..., D_MEMH:], N_MEM_HEADS)
    s = jnp.einsum('bthd,bmhd->bhtm', q, k).astype(jnp.float32) * ATTN_SCALE
    p = jax.nn.softmax(s, axis=-1).astype(v.dtype)
    o = jnp.einsum('bhtm,bmhd->bthd', p, v)
    return o.reshape(bsz, t, D_MEMH)


def rwkv7_time_mix(u, mu, w0, w_up, a0, a_up, g_up, k_k, k_a, r_k, lnx_g, lnx_b):
    bsz, t = u.shape[:2]
    f32 = jnp.float32
    us = u + (token_shift(u) - u) * mu
    o = 0
    r = us[..., o:o + D_MAIN]; o += D_MAIN
    k = us[..., o:o + D_MAIN]; o += D_MAIN
    v = us[..., o:o + D_MAIN]; o += D_MAIN
    wd = us[..., o:o + D_DECAY_LORA]; o += D_DECAY_LORA
    ad = us[..., o:o + D_AAA_LORA]; o += D_AAA_LORA
    gd = us[..., o:o + D_GATE_LORA]
    w_log = -jax.nn.softplus(-(w0 + jnp.tanh(wd) @ w_up)) - 0.5
    decay = jnp.exp(-jnp.exp(w_log.astype(f32)))
    a = jax.nn.sigmoid(a0 + ad @ a_up)
    g = jax.nn.sigmoid(gd) @ g_up
    kk = split_heads((k * k_k).astype(f32), N_MAIN_HEADS)
    kk = kk * lax.rsqrt(jnp.maximum(jnp.sum(kk * kk, -1, keepdims=True), 1e-24))
    k = k * (1 + (a - 1) * k_a)
    rh, wh, kh, vh, ah = (split_heads(z.astype(f32), N_MAIN_HEADS) for z in (r, decay, k, v, a))

    def step(state, inp):
        r_t, w_t, k_t, v_t, kk_t, a_t = inp
        sa = jnp.einsum('bhvk,bhk->bhv', state, -kk_t)
        state = (state * w_t[:, :, None, :]
                 + sa[..., None] * (kk_t * a_t)[:, :, None, :]
                 + v_t[..., None] * k_t[:, :, None, :])
        return state, jnp.einsum('bhvk,bhk->bhv', state, r_t)

    xs = tuple(jnp.moveaxis(z, 1, 0) for z in (rh, wh, kh, vh, kk, ah))
    s0 = jnp.zeros((bsz, N_MAIN_HEADS, HEAD_DIM, HEAD_DIM), f32)
    _, y = lax.scan(step, s0, xs)
    y = jnp.moveaxis(y, 0, 1)
    mean = jnp.mean(y, -1, keepdims=True)
    var = jnp.mean(jnp.square(y - mean), -1, keepdims=True)
    yn = ((y - mean) * lax.rsqrt(var + GN_EPS)).reshape(bsz, t, D_MAIN)
    yn = yn * lnx_g.astype(f32) + lnx_b.astype(f32)
    bonus = jnp.sum(rh * kh * r_k.astype(f32), -1, keepdims=True) * vh
    out = (yn + bonus.reshape(bsz, t, D_MAIN)) * g.astype(f32)
    return out.astype(u.dtype)


def shared_kv(h, kv_norm_g, w_kv, b_f, k_gain):
    kvf = rms_norm(h, kv_norm_g) @ w_kv
    k = rms_norm(split_heads(kvf[..., :D_MAIN], N_MAIN_HEADS), k_gain)
    v = split_heads(kvf[..., D_MAIN:2 * D_MAIN], N_MAIN_HEADS)
    log_f = jax.nn.log_sigmoid((kvf[..., 2 * D_MAIN:] + b_f).astype(jnp.float32))
    c = jnp.cumsum(log_f, axis=1)
    return k, v, c


def forgetting_attention(q, k, v, c):
    t = q.shape[1]
    c_h = jnp.transpose(c, (0, 2, 1))
    outs = []
    for blk in range(t // BLOCK_Q):
        q0 = blk * BLOCK_Q
        q1 = q0 + BLOCK_Q
        s = jnp.einsum('bqhd,bkhd->bhqk', q[:, q0:q1], k[:, :q1]).astype(jnp.float32) * ATTN_SCALE
        s = s + c_h[:, :, q0:q1, None] - c_h[:, :, None, :q1]
        causal = jnp.arange(q1)[None, :] <= (q0 + jnp.arange(BLOCK_Q))[:, None]
        s = jnp.where(causal, s, NEG_INF)
        p = jax.nn.softmax(s, axis=-1).astype(v.dtype)
        outs.append(jnp.einsum('bhqk,bkhd->bqhd', p, v[:, :q1]))
    return jnp.concatenate(outs, axis=1)


def setup_inputs(seed: int = 0) -> dict:
    key = jax.random.key(seed)
    ks = iter(jax.random.split(key, 40))
    f32 = jnp.float32

    def nrm(shape, scale):
        return jax.random.normal(next(ks), shape, f32) * scale

    def gain(shape):
        return 1.0 + nrm(shape, 0.02)

    d = D_MODEL
    return {
        "x": nrm((BATCH, SEQ, d), 1.0),
        "mem": nrm((BATCH, MEM_LEN, d), 1.0),
        "mix_norm": gain((DEPTH, d)),
        "w_out": nrm((DEPTH, D_MIX, d), 0.5 * D_MIX ** -0.5),
        "mem_norm": gain((DEPTH, d)),
        "w_mem_kv": nrm((DEPTH, d, 2 * D_MEMH), d ** -0.5),
        "mem_q_gain": gain((DEPTH, HEAD_DIM)),
        "mem_k_gain": gain((DEPTH, HEAD_DIM)),
        "ffn_norm": gain((DEPTH, d)),
        "w_gate_up": nrm((DEPTH, d, 2 * D_FF), d ** -0.5),
        "w_down": nrm((DEPTH, D_FF, d), 0.5 * D_FF ** -0.5),
        "a_w_in": nrm((N_A, d, D_IN_A), d ** -0.5),
        "a_mu": jax.random.uniform(next(ks), (N_A, D_SHIFT), f32),
        "a_w0": nrm((N_A, D_MAIN), 0.5),
        "a_w_up": nrm((N_A, D_DECAY_LORA, D_MAIN), 0.1),
        "a_a0": nrm((N_A, D_MAIN), 0.5),
        "a_a_up": nrm((N_A, D_AAA_LORA, D_MAIN), 0.5 * D_AAA_LORA ** -0.5),
        "a_g_up": nrm((N_A, D_GATE_LORA, D_MAIN), D_GATE_LORA ** -0.5),
        "a_k_k": 0.85 + nrm((N_A, D_MAIN), 0.05),
        "a_k_a": 1.0 + nrm((N_A, D_MAIN), 0.05),
        "a_r_k": nrm((N_A, N_MAIN_HEADS, HEAD_DIM), 0.1),
        "a_lnx_g": gain((N_A, D_MAIN)),
        "a_lnx_b": nrm((N_A, D_MAIN), 0.02),
        "kv_norm": gain((d,)),
        "w_kv": nrm((d, D_KV), d ** -0.5),
        "b_f": 2.0 + nrm((N_MAIN_HEADS,), 0.5),
        "k_gain": gain((HEAD_DIM,)),
        "b_w_in": nrm((N_B, d, D_IN_B), d ** -0.5),
        "b_q_gain": gain((N_B, HEAD_DIM)),
    }


def reference(x, mem, mix_norm, w_out, mem_norm, w_mem_kv, mem_q_gain, mem_k_gain,
              ffn_norm, w_gate_up, w_down, a_w_in, a_mu, a_w0, a_w_up, a_a0, a_a_up,
              a_g_up, a_k_k, a_k_a, a_r_k, a_lnx_g, a_lnx_b, kv_norm, w_kv, b_f, k_gain,
              b_w_in, b_q_gain):
    bsz, t = x.shape[:2]

    def merge_and_ffn(x, l, main_out, mem_cols):
        mem_out = memory_attention(mem_cols, mem, mem_norm[l], w_mem_kv[l], mem_q_gain[l], mem_k_gain[l])
        x = x + jnp.concatenate([main_out, mem_out], axis=-1) @ w_out[l]
        return x + swiglu(rms_norm(x, ffn_norm[l]), w_gate_up[l], w_down[l])

    for i in range(N_A):
        l = i
        u = rms_norm(x, mix_norm[l]) @ a_w_in[i]
        main = rwkv7_time_mix(u[..., :D_SHIFT], a_mu[i], a_w0[i], a_w_up[i], a_a0[i], a_a_up[i],
                              a_g_up[i], a_k_k[i], a_k_a[i], a_r_k[i], a_lnx_g[i], a_lnx_b[i])
        x = merge_and_ffn(x, l, main, u[..., D_SHIFT:])

    k_sh, v_sh, c_sh = shared_kv(x, kv_norm, w_kv, b_f, k_gain)

    for j in range(N_B):
        l = N_A + j
        u = rms_norm(x, mix_norm[l]) @ b_w_in[j]
        q = rms_norm(split_heads(u[..., :D_MAIN], N_MAIN_HEADS), b_q_gain[j])
        o = forgetting_attention(q, k_sh, v_sh, c_sh).reshape(bsz, t, D_MAIN)
        main = o * jax.nn.sigmoid(u[..., D_MAIN:2 * D_MAIN])
        x = merge_and_ffn(x, l, main, u[..., 2 * D_MAIN:])

    return x
```

```python
import functools

import jax
import jax.numpy as jnp
from jax import lax
from jax.experimental import pallas as pl
from jax.experimental.pallas import tpu as pltpu

F32 = jnp.float32
BF16 = jnp.bfloat16
HIGHEST = lax.Precision.HIGHEST

HEAD_DIM = 64
LANES = 128
N_MAIN_HEADS = 12
N_PAIRS = N_MAIN_HEADS // 2
D_MAIN = N_MAIN_HEADS * HEAD_DIM
N_MEM_HEADS = 4
D_MEMH = N_MEM_HEADS * HEAD_DIM
D_DECAY_LORA = 64
D_AAA_LORA = 64
D_GATE_LORA = 160
D_LORA_WA = LANES
D_LORA_G = 2 * LANES
D_A_MAIN = 3 * D_MAIN + D_LORA_WA + D_LORA_G
RMS_EPS = 1e-6
GN_EPS = 64e-5
ATTN_SCALE = HEAD_DIM ** -0.5
NEG_INF = -1e30
VMEM_LIMIT = 56 * 1024 * 1024

RWKV_CHUNK = 64
ROW_BLOCK = 512
ATTN_BLOCK = 512
CUMSUM_BLOCK = 128
FF_CHUNK = 1408


def _params(*sem):
    return pltpu.CompilerParams(dimension_semantics=sem, vmem_limit_bytes=VMEM_LIMIT)


def _const_spec(shape):
    nd = len(shape)
    return pl.BlockSpec(shape, lambda *_: (0,) * nd)


def _dot(a, b):
    return jnp.dot(a.astype(BF16), b.astype(BF16), preferred_element_type=F32)


def _dot_nt(a, b):
    return lax.dot_general(a.astype(BF16), b.astype(BF16), (((1,), (1,)), ((), ())),
                           preferred_element_type=F32)


def _dot_tn(a, b):
    return lax.dot_general(a.astype(BF16), b.astype(BF16), (((0,), (0,)), ((), ())),
                           preferred_element_type=F32)


def _dot_f32(a, b):
    return jnp.dot(a, b, preferred_element_type=F32, precision=HIGHEST)


def _sigmoid(x):
    return 1.0 / (1.0 + jnp.exp(-x))


def _softplus(x):
    return jnp.maximum(x, 0.0) + jnp.log(1.0 + jnp.exp(-jnp.abs(x)))


def _rms(x, g):
    return x * lax.rsqrt(jnp.mean(x * x, axis=-1, keepdims=True) + RMS_EPS) * g


def _lo_mask(shape):
    return lax.broadcasted_iota(jnp.int32, shape, len(shape) - 1) < HEAD_DIM


def _pair_sum(x, lo):
    s_lo = jnp.sum(jnp.where(lo, x, 0.0), axis=-1, keepdims=True)
    s_hi = jnp.sum(jnp.where(lo, 0.0, x), axis=-1, keepdims=True)
    return jnp.where(lo, s_lo, s_hi)


def _head_rms(x, gain):
    outs = []
    for j in range(x.shape[-1] // LANES):
        xb = x[:, j * LANES:(j + 1) * LANES]
        lo = _lo_mask(xb.shape)
        ms = _pair_sum(xb * xb, lo) * (1.0 / HEAD_DIM)
        outs.append(xb * lax.rsqrt(ms + RMS_EPS))
    y = outs[0] if len(outs) == 1 else jnp.concatenate(outs, axis=-1)
    return y * gain


def _tril(n, strict):
    r = lax.broadcasted_iota(jnp.int32, (n, n), 0)
    c = lax.broadcasted_iota(jnp.int32, (n, n), 1)
    return (r > c) if strict else (r >= c)


def _norm_proj_kernel(x_ref, g_ref, w_ref, *o_refs, splits):
    h = _rms(x_ref[...], g_ref[...]).astype(BF16)
    c0 = 0
    for o_ref, width in zip(o_refs, splits):
        for j in range(0, width, 2 * LANES):
            wj = min(2 * LANES, width - j)
            o_ref[:, j:j + wj] = jnp.dot(h, w_ref[:, c0 + j:c0 + j + wj],
                                         preferred_element_type=F32).astype(o_ref.dtype)
        c0 += width


def _norm_proj(x, g, w, splits, dtypes):
    n, d = x.shape
    tm = ROW_BLOCK
    return pl.pallas_call(
        functools.partial(_norm_proj_kernel, splits=splits),
        out_shape=[jax.ShapeDtypeStruct((n, s), dt) for s, dt in zip(splits, dtypes)],
        grid=(n // tm,),
        in_specs=[pl.BlockSpec((tm, d), lambda i: (i, 0)),
                  _const_spec((1, d)),
                  _const_spec(w.shape)],
        out_specs=[pl.BlockSpec((tm, s), lambda i: (i, 0)) for s in splits],
        compiler_params=_params("parallel"),
        name="norm_proj",
    )(x, g, w)


def _mem_kv_kernel(mem_ref, g_ref, w_ref, kg_ref, k_ref, v_ref):
    h = _rms(mem_ref[0], g_ref[0]).astype(BF16)
    kv = jnp.dot(h, w_ref[0], preferred_element_type=F32)
    k_ref[0, 0] = _head_rms(kv[:, :D_MEMH], kg_ref[0]).astype(BF16)
    v_ref[0, 0] = kv[:, D_MEMH:].astype(BF16)


def _mem_kv(mem, mem_norm, w_mem_kv, mem_k_gain):
    b, m, d = mem.shape
    depth = w_mem_kv.shape[0]
    out = jax.ShapeDtypeStruct((depth, b, m, D_MEMH), BF16)
    return pl.pallas_call(
        _mem_kv_kernel,
        out_shape=[out, out],
        grid=(depth, b),
        in_specs=[pl.BlockSpec((1, m, d), lambda l, i: (i, 0, 0)),
                  pl.BlockSpec((1, 1, d), lambda l, i: (l, 0, 0)),
                  pl.BlockSpec((1, d, 2 * D_MEMH), lambda l, i: (l, 0, 0)),
                  pl.BlockSpec((1, 1, D_MEMH), lambda l, i: (l, 0, 0))],
        out_specs=[pl.BlockSpec((1, 1, m, D_MEMH), lambda l, i: (l, i, 0, 0))] * 2,
        compiler_params=_params("parallel", "parallel"),
        name="mem_kv",
    )(mem, mem_norm[:, None, :], w_mem_kv.astype(BF16),
      jnp.tile(mem_k_gain, (1, N_MEM_HEADS))[:, None, :])


def _mem_attn_kernel(q_ref, k_ref, v_ref, qg_ref, o_ref):
    qn = _head_rms(q_ref[0], qg_ref[...]) * ATTN_SCALE
    k = k_ref[0, 0]
    v = v_ref[0, 0]
    lane = lax.broadcasted_iota(jnp.int32, qn.shape, 1)
    out = jnp.zeros(qn.shape, F32)
    for h in range(N_MEM_HEADS):
        sel = jnp.logical_and(lane >= h * HEAD_DIM, lane < (h + 1) * HEAD_DIM)
        s = _dot_nt(jnp.where(sel, qn, 0.0), k)
        e = jnp.exp(s - jnp.max(s, axis=-1, keepdims=True))
        o = _dot(e, v) / jnp.sum(e, axis=-1, keepdims=True)
        out = jnp.where(sel, o, out)
    o_ref[0] = out.astype(o_ref.dtype)


def _mem_attn(q_cols, k_mem, v_mem, q_gain, layer, bsz):
    n, _ = q_cols.shape
    t = n // bsz
    tq = ATTN_BLOCK
    m = k_mem.shape[2]
    kv_spec = pl.BlockSpec((1, 1, m, D_MEMH), lambda b, i: (layer, b, 0, 0))
    out = pl.pallas_call(
        _mem_attn_kernel,
        out_shape=jax.ShapeDtypeStruct((bsz, t, D_MEMH), BF16),
        grid=(bsz, t // tq),
        in_specs=[pl.BlockSpec((1, tq, D_MEMH), lambda b, i: (b, i, 0)),
                  kv_spec, kv_spec, _const_spec((1, D_MEMH))],
        out_specs=pl.BlockSpec((1, tq, D_MEMH), lambda b, i: (b, i, 0)),
        compiler_params=_params("parallel", "parallel"),
        name="mem_attn",
    )(q_cols.reshape(bsz, t, D_MEMH), k_mem, v_mem, jnp.tile(q_gain, N_MEM_HEADS)[None, :])
    return out.reshape(n, D_MEMH)


def _unit_lower_inverse(a_strict):
    n = a_strict.shape[0]
    eye = (lax.broadcasted_iota(jnp.int32, (n, n), 0)
           == lax.broadcasted_iota(jnp.int32, (n, n), 1)).astype(F32)
    inv = eye + a_strict
    power = a_strict
    for _ in range(n.bit_length() - 2):
        power = _dot(power, power)
        inv = inv + _dot(power, inv)
    return inv


def _rwkv_kernel(u_ref, mu_ref, wup_ref, aup_ref, gup_ref, w0_ref, a0_ref, kk_ref, ka_ref,
                 rk_ref, lng_ref, lnb_ref, o_ref, state_ref, prev_ref):
    L = u_ref.shape[0]
    c = pl.program_id(1)

    @pl.when(c == 0)
    def _():
        state_ref[...] = jnp.zeros_like(state_ref)
        prev_ref[...] = jnp.zeros_like(prev_ref)

    u = u_ref[...]
    row = lax.broadcasted_iota(jnp.int32, u.shape, 0)
    shifted = jnp.where(row == 0, prev_ref[...], pltpu.roll(u, 1, axis=0))
    prev_ref[...] = u[L - 1:L, :]
    us = u + (shifted - u) * mu_ref[...]

    r = us[:, 0:D_MAIN]
    k = us[:, D_MAIN:2 * D_MAIN]
    v = us[:, 2 * D_MAIN:3 * D_MAIN]
    x_wa = us[:, 3 * D_MAIN:3 * D_MAIN + D_LORA_WA]
    x_g = us[:, 3 * D_MAIN + D_LORA_WA:]

    w_log = -_softplus(-(w0_ref[...] + _dot(jnp.tanh(x_wa), wup_ref[...]))) - 0.5
    log_w = -jnp.exp(w_log)
    lr = _sigmoid(a0_ref[...] + _dot(x_wa, aup_ref[...]))
    gate = _dot(_sigmoid(x_g), gup_ref[...])
    kk_raw = k * kk_ref[...]
    k = k * (1.0 + (lr - 1.0) * ka_ref[...])

    cum = _dot_f32(_tril(L, strict=False).astype(F32), log_w)
    cum_end = cum[L - 1:L, :]
    g_incl = jnp.exp(cum)
    g_excl = jnp.exp(cum - log_w)
    g_inv = jnp.exp(-cum)
    g_tail = jnp.exp(cum_end - cum)

    strict = _tril(L, strict=True)
    incl = _tril(L, strict=False)
    lo = _lo_mask((L, LANES))
    lo2 = _lo_mask((2 * L, LANES))
    vrow = lax.broadcasted_iota(jnp.int32, (LANES, LANES), 0) < HEAD_DIM
    block_diag = vrow == _lo_mask((LANES, LANES))

    for p in range(N_PAIRS):
        sl = slice(p * LANES, (p + 1) * LANES)
        kk = kk_raw[:, sl]
        kk = kk * lax.rsqrt(jnp.maximum(_pair_sum(kk * kk, lo), 1e-24))
        a_t = -kk * g_excl[:, sl]
        r_t = r[:, sl] * g_incl[:, sl]
        b = kk * lr[:, sl]
        b_t = b * g_inv[:, sl]
        k_t = k[:, sl] * g_inv[:, sl]
        vp = v[:, sl]
        ar = jnp.concatenate([a_t, r_t], axis=0)
        state = state_ref[p]
        from_state = _dot_nt(ar, state)

        a_ak, a_rb, a_rk, inv = [], [], [], []
        for j in range(2):
            sel = lo2 if j == 0 else jnp.logical_not(lo2)
            ar_j = jnp.where(sel, ar, 0.0)
            gb = _dot_nt(ar_j, b_t)
            gk = _dot_nt(ar_j, k_t)
            inv.append(_unit_lower_inverse(jnp.where(strict, gb[:L], 0.0)))
            a_ak.append(jnp.where(strict, gk[:L], 0.0))
            a_rb.append(jnp.where(incl, gb[L:], 0.0))
            a_rk.append(jnp.where(incl, gk[L:], 0.0))

        w_mat = from_state[:L] + jnp.where(lo, _dot(a_ak[0], vp), _dot(a_ak[1], vp))
        u_mat = jnp.where(lo, _dot(inv[0], w_mat), _dot(inv[1], w_mat))
        y = from_state[L:] + jnp.where(lo, _dot(a_rb[0], u_mat) + _dot(a_rk[0], vp),
                                       _dot(a_rb[1], u_mat) + _dot(a_rk[1], vp))

        uv = jnp.concatenate([u_mat, vp], axis=0)
        bk = jnp.concatenate([b, k[:, sl]], axis=0) * jnp.concatenate(
            [g_tail[:, sl], g_tail[:, sl]], axis=0)
        state_ref[p] = state * g_incl[L - 1:L, sl] + jnp.where(block_diag, _dot_tn(uv, bk), 0.0)

        mean = _pair_sum(y, lo) * (1.0 / HEAD_DIM)
        yc = y - mean
        var = _pair_sum(yc * yc, lo) * (1.0 / HEAD_DIM)
        yn = yc * lax.rsqrt(var + GN_EPS) * lng_ref[:, sl] + lnb_ref[:, sl]
        bonus = _pair_sum(r[:, sl] * k[:, sl] * rk_ref[:, sl], lo) * vp
        o_ref[:, sl] = ((yn + bonus) * gate[:, sl]).astype(o_ref.dtype)


def _rwkv(u_main, bsz, mu, w_up, a_up, g_up, w0, a0, k_k, k_a, r_k, lnx_g, lnx_b):
    n, width = u_main.shape
    nc = n // bsz // RWKV_CHUNK
    rows = [w0, a0, k_k, k_a, r_k.reshape(-1), lnx_g, lnx_b]
    zeros_wa = jnp.zeros((D_DECAY_LORA, D_MAIN), F32)
    wup = jnp.concatenate([w_up, zeros_wa], axis=0)
    aup = jnp.concatenate([zeros_wa, a_up], axis=0)
    gup = jnp.concatenate([g_up, jnp.zeros((D_LORA_G - D_GATE_LORA, D_MAIN), F32)], axis=0)
    return pl.pallas_call(
        _rwkv_kernel,
        out_shape=jax.ShapeDtypeStruct((n, D_MAIN), BF16),
        grid=(bsz, nc),
        in_specs=[pl.BlockSpec((RWKV_CHUNK, width), lambda b, c: (b * nc + c, 0)),
                  _const_spec((1, width)),
                  _const_spec(wup.shape), _const_spec(aup.shape), _const_spec(gup.shape)]
                 + [_const_spec((1, D_MAIN))] * len(rows),
        out_specs=pl.BlockSpec((RWKV_CHUNK, D_MAIN), lambda b, c: (b * nc + c, 0)),
        scratch_shapes=[pltpu.VMEM((N_PAIRS, LANES, LANES), F32),
                        pltpu.VMEM((1, width), F32)],
        compiler_params=_params("parallel", "arbitrary"),
        name="rwkv7",
    )(u_main, mu[None, :], wup, aup, gup, *[x[None, :] for x in rows])


def _shared_kv_kernel(x_ref, g_ref, w_ref, bf_ref, kg_ref, k_ref, v_ref, c_ref, carry_ref):
    @pl.when(pl.program_id(1) == 0)
    def _():
        carry_ref[...] = jnp.zeros_like(carry_ref)

    h = _rms(x_ref[...], g_ref[...]).astype(BF16)
    for j in range(0, D_MAIN, 2 * LANES):
        kj = jnp.dot(h, w_ref[:, j:j + 2 * LANES], preferred_element_type=F32)
        k_ref[:, j:j + 2 * LANES] = _head_rms(kj, kg_ref[:, j:j + 2 * LANES]).astype(BF16)
        v_ref[:, j:j + 2 * LANES] = jnp.dot(
            h, w_ref[:, D_MAIN + j:D_MAIN + j + 2 * LANES], preferred_element_type=F32).astype(BF16)
    logits = jnp.dot(h, w_ref[:, 2 * D_MAIN:], preferred_element_type=F32) + bf_ref[...]
    log_f = -_softplus(-logits)
    tri = _tril(CUMSUM_BLOCK, strict=False).astype(F32)
    carry = carry_ref[...]
    for j in range(0, x_ref.shape[0], CUMSUM_BLOCK):
        cj = _dot_f32(tri, log_f[j:j + CUMSUM_BLOCK]) + carry
        c_ref[j:j + CUMSUM_BLOCK, :] = cj
        carry = cj[CUMSUM_BLOCK - 1:CUMSUM_BLOCK, :]
    carry_ref[...] = carry


def _shared_kv(x, bsz, kv_norm, w_kv, b_f, k_gain):
    n, d = x.shape
    tm = ROW_BLOCK
    nt = n // bsz // tm
    pad = LANES - N_MAIN_HEADS
    w = jnp.pad(w_kv, ((0, 0), (0, pad))).astype(BF16)
    row = lambda b, i: (b * nt + i, 0)
    return pl.pallas_call(
        _shared_kv_kernel,
        out_shape=[jax.ShapeDtypeStruct((n, D_MAIN), BF16),
                   jax.ShapeDtypeStruct((n, D_MAIN), BF16),
                   jax.ShapeDtypeStruct((n, LANES), F32)],
        grid=(bsz, nt),
        in_specs=[pl.BlockSpec((tm, d), row), _const_spec((1, d)), _const_spec(w.shape),
                  _const_spec((1, LANES)), _const_spec((1, D_MAIN))],
        out_specs=[pl.BlockSpec((tm, D_MAIN), row), pl.BlockSpec((tm, D_MAIN), row),
                   pl.BlockSpec((tm, LANES), row)],
        scratch_shapes=[pltpu.VMEM((1, LANES), F32)],
        compiler_params=_params("parallel", "arbitrary"),
        name="shared_kv",
    )(x, kv_norm[None, :], w, jnp.pad(b_f, (0, pad))[None, :],
      jnp.tile(k_gain, N_MAIN_HEADS)[None, :])


def _fox_kernel(q_ref, k_ref, v_ref, crow_ref, ccol_ref, gate_ref, qg_ref, o_ref):
    tq = q_ref.shape[1]
    p = pl.program_id(1)
    qi = pl.program_id(2)
    qn = _head_rms(q_ref[0], qg_ref[...]) * ATTN_SCALE
    lo = _lo_mask(qn.shape)
    c_cols = ccol_ref[0]
    lane = lax.broadcasted_iota(jnp.int32, c_cols.shape, 1)
    causal = _tril(tq, strict=False)
    heads = []
    for j in range(2):
        sel = lo if j == 0 else jnp.logical_not(lo)
        qj = jnp.where(sel, qn, 0.0).astype(BF16)
        cq = jnp.sum(jnp.where(lane == 2 * p + j, c_cols, 0.0), axis=-1, keepdims=True)

        def scores(start):
            kb = k_ref[0, pl.ds(start, tq), :]
            ck = crow_ref[0, 0, j:j + 1, pl.ds(start, tq)]
            return _dot_nt(qj, kb) + (cq - ck)

        def update(s, start, carry):
            m, l, acc = carry
            m_new = jnp.maximum(m, jnp.max(s, axis=-1, keepdims=True))
            alpha = jnp.exp(m - m_new)
            e = jnp.exp(s - m_new)
            l = alpha * l + jnp.sum(e, axis=-1, keepdims=True)
            acc = alpha * acc + _dot(e, v_ref[0, pl.ds(start, tq), :])
            return m_new, l, acc

        def body(kb, carry):
            start = pl.multiple_of(kb * tq, tq)
            return update(scores(start), start, carry)

        init = (jnp.full((tq, 1), NEG_INF, F32), jnp.zeros((tq, 1), F32),
                jnp.zeros((tq, LANES), F32))
        carry = lax.fori_loop(0, qi, body, init)
        diag = pl.multiple_of(qi * tq, tq)
        _, l, acc = update(jnp.where(causal, scores(diag), NEG_INF), diag, carry)
        heads.append(acc / l)
    o = jnp.where(lo, heads[0], heads[1])
    o_ref[0] = (o * _sigmoid(gate_ref[0])).astype(o_ref.dtype)


def _fox(q, gate, k, v, c_cols, q_gain, bsz):
    n, _ = q.shape
    t = n // bsz
    tq = ATTN_BLOCK
    c_rows = jnp.transpose(c_cols.reshape(bsz, t, LANES)[:, :, :N_MAIN_HEADS], (0, 2, 1))
    c_rows = c_rows.reshape(bsz, N_PAIRS, 2, t)
    blk = pl.BlockSpec((1, tq, LANES), lambda b, p, i: (b, i, p))
    full = pl.BlockSpec((1, t, LANES), lambda b, p, i: (b, 0, p))
    out = pl.pallas_call(
        _fox_kernel,
        out_shape=jax.ShapeDtypeStruct((bsz, t, D_MAIN), BF16),
        grid=(bsz, N_PAIRS, t // tq),
        in_specs=[blk, full, full,
                  pl.BlockSpec((1, 1, 2, t), lambda b, p, i: (b, p, 0, 0)),
                  pl.BlockSpec((1, tq, LANES), lambda b, p, i: (b, i, 0)),
                  blk, _const_spec((1, LANES))],
        out_specs=blk,
        compiler_params=_params("parallel", "parallel", "arbitrary"),
        name="fox_attn",
    )(q.reshape(bsz, t, D_MAIN), k.reshape(bsz, t, D_MAIN), v.reshape(bsz, t, D_MAIN),
      c_rows, c_cols.reshape(bsz, t, LANES), gate.reshape(bsz, t, D_MAIN),
      jnp.tile(q_gain, 2)[None, :])
    return out.reshape(n, D_MAIN)


def _merge_ffn_kernel(x_ref, main_ref, memo_ref, wo_ref, g_ref, wgu_ref, wd_ref, o_ref):
    d_ff = wd_ref.shape[0]
    mixed = (jnp.dot(main_ref[...], wo_ref[:D_MAIN, :], preferred_element_type=F32)
             + jnp.dot(memo_ref[...], wo_ref[D_MAIN:, :], preferred_element_type=F32))
    x = x_ref[...] + mixed
    h = _rms(x, g_ref[...]).astype(BF16)
    acc = x
    for j in range(0, d_ff, FF_CHUNK):
        gj = jnp.dot(h, wgu_ref[:, j:j + FF_CHUNK], preferred_element_type=F32)
        uj = jnp.dot(h, wgu_ref[:, d_ff + j:d_ff + j + FF_CHUNK], preferred_element_type=F32)
        act = (gj * _sigmoid(gj) * uj).astype(BF16)
        acc = acc + jnp.dot(act, wd_ref[j:j + FF_CHUNK, :], preferred_element_type=F32)
    o_ref[...] = acc


def _merge_ffn(x, main, memo, w_out, ffn_norm, w_gate_up, w_down):
    n, d = x.shape
    tm = ROW_BLOCK
    row = lambda i: (i, 0)
    single = pl.Buffered(1)
    return pl.pallas_call(
        _merge_ffn_kernel,
        out_shape=jax.ShapeDtypeStruct((n, d), F32),
        grid=(n // tm,),
        in_specs=[pl.BlockSpec((tm, d), row), pl.BlockSpec((tm, D_MAIN), row),
                  pl.BlockSpec((tm, D_MEMH), row),
                  pl.BlockSpec(w_out.shape, lambda i: (0, 0), pipeline_mode=single),
                  _const_spec((1, d)),
                  pl.BlockSpec(w_gate_up.shape, lambda i: (0, 0), pipeline_mode=single),
                  pl.BlockSpec(w_down.shape, lambda i: (0, 0), pipeline_mode=single)],
        out_specs=pl.BlockSpec((tm, d), row),
        compiler_params=_params("parallel"),
        name="merge_ffn",
    )(x, main, memo, w_out.astype(BF16), ffn_norm[None, :], w_gate_up.astype(BF16),
      w_down.astype(BF16))


def _a_in_weights(w_in, mu):
    d_shift = 3 * D_MAIN + D_DECAY_LORA + D_AAA_LORA + D_GATE_LORA
    pad = D_LORA_G - D_GATE_LORA
    w = jnp.concatenate([w_in[:, :d_shift], jnp.zeros((w_in.shape[0], pad), w_in.dtype),
                         w_in[:, d_shift:]], axis=1)
    return w.astype(BF16), jnp.pad(mu, (0, pad))


def kernel(x, mem, mix_norm, w_out, mem_norm, w_mem_kv, mem_q_gain, mem_k_gain, ffn_norm, w_gate_up, w_down, a_w_in, a_mu, a_w0, a_w_up, a_a0, a_a_up, a_g_up, a_k_k, a_k_a, a_r_k, a_lnx_g, a_lnx_b, kv_norm, w_kv, b_f, k_gain, b_w_in, b_q_gain):
    bsz, t, d = x.shape
    n_a = a_w_in.shape[0]
    n_b = b_w_in.shape[0]
    x = x.reshape(bsz * t, d)
    k_mem, v_mem = _mem_kv(mem, mem_norm, w_mem_kv, mem_k_gain)

    def merge(x, l, main, mem_q):
        memo = _mem_attn(mem_q, k_mem, v_mem, mem_q_gain[l], l, bsz)
        return _merge_ffn(x, main, memo, w_out[l], ffn_norm[l], w_gate_up[l], w_down[l])

    for i in range(n_a):
        w, mu = _a_in_weights(a_w_in[i], a_mu[i])
        u_main, mem_q = _norm_proj(x, mix_norm[i][None, :], w, (D_A_MAIN, D_MEMH), (F32, F32))
        main = _rwkv(u_main, bsz, mu, a_w_up[i], a_a_up[i], a_g_up[i], a_w0[i], a_a0[i],
                     a_k_k[i], a_k_a[i], a_r_k[i], a_lnx_g[i], a_lnx_b[i])
        x = merge(x, i, main, mem_q)

    k_sh, v_sh, c_sh = _shared_kv(x, bsz, kv_norm, w_kv, b_f, k_gain)

    for j in range(n_b):
        l = n_a + j
        q, gate, mem_q = _norm_proj(x, mix_norm[l][None, :], b_w_in[j].astype(BF16),
                                    (D_MAIN, D_MAIN, D_MEMH), (F32, F32, F32))
        main = _fox(q, gate, k_sh, v_sh, c_sh, b_q_gain[j], bsz)
        x = merge(x, l, main, mem_q)

    return x.reshape(bsz, t, d)
```

```python
import functools

import jax
import jax.numpy as jnp
from jax import lax
from jax.experimental import pallas as pl
from jax.experimental.pallas import tpu as pltpu

F32 = jnp.float32
BF16 = jnp.bfloat16
HIGHEST = lax.Precision.HIGHEST

HEAD_DIM = 64
LANES = 128
N_MAIN_HEADS = 12
N_PAIRS = N_MAIN_HEADS // 2
D_MAIN = N_MAIN_HEADS * HEAD_DIM
N_MEM_HEADS = 4
D_MEMH = N_MEM_HEADS * HEAD_DIM
D_DECAY_LORA = 64
D_AAA_LORA = 64
D_GATE_LORA = 160
D_LORA_WA = LANES
D_LORA_G = 2 * LANES
D_A_MAIN = 3 * D_MAIN + D_LORA_WA + D_LORA_G
RMS_EPS = 1e-6
GN_EPS = 64e-5
ATTN_SCALE = HEAD_DIM ** -0.5
NEG_INF = -1e30
LOG2E = 1.4426950408889634
VMEM_LIMIT = 56 * 1024 * 1024

RWKV_CHUNK = 64
ROW_BLOCK = 512
ATTN_BLOCK = 512
CUMSUM_BLOCK = 128
FF_CHUNK = 1408


def _params(*sem):
    return pltpu.CompilerParams(dimension_semantics=sem, vmem_limit_bytes=VMEM_LIMIT)


def _const_spec(shape):
    nd = len(shape)
    return pl.BlockSpec(shape, lambda *_: (0,) * nd)


def _dot(a, b):
    return jnp.dot(a.astype(BF16), b.astype(BF16), preferred_element_type=F32)


def _dot_nt(a, b):
    return lax.dot_general(a.astype(BF16), b.astype(BF16), (((1,), (1,)), ((), ())),
                           preferred_element_type=F32)


def _dot_tn(a, b):
    return lax.dot_general(a.astype(BF16), b.astype(BF16), (((0,), (0,)), ((), ())),
                           preferred_element_type=F32)


def _dot_f32(a, b):
    return jnp.dot(a, b, preferred_element_type=F32, precision=HIGHEST)


def _sigmoid(x):
    return 1.0 / (1.0 + jnp.exp(-x))


def _softplus(x):
    return jnp.maximum(x, 0.0) + jnp.log(1.0 + jnp.exp(-jnp.abs(x)))


def _rms(x, g):
    return x * lax.rsqrt(jnp.mean(x * x, axis=-1, keepdims=True) + RMS_EPS) * g


def _lo_mask(shape):
    return lax.broadcasted_iota(jnp.int32, shape, len(shape) - 1) < HEAD_DIM


def _pair_sum(x, lo):
    s_lo = jnp.sum(jnp.where(lo, x, 0.0), axis=-1, keepdims=True)
    s_hi = jnp.sum(jnp.where(lo, 0.0, x), axis=-1, keepdims=True)
    return jnp.where(lo, s_lo, s_hi)


def _head_rms(x, gain):
    outs = []
    for j in range(x.shape[-1] // LANES):
        xb = x[:, j * LANES:(j + 1) * LANES]
        lo = _lo_mask(xb.shape)
        ms = _pair_sum(xb * xb, lo) * (1.0 / HEAD_DIM)
        outs.append(xb * lax.rsqrt(ms + RMS_EPS))
    y = outs[0] if len(outs) == 1 else jnp.concatenate(outs, axis=-1)
    return y * gain


def _tril(n, strict):
    r = lax.broadcasted_iota(jnp.int32, (n, n), 0)
    c = lax.broadcasted_iota(jnp.int32, (n, n), 1)
    return (r > c) if strict else (r >= c)


def _norm_proj_kernel(x_ref, g_ref, w_ref, *o_refs, splits):
    h = _rms(x_ref[...], g_ref[...]).astype(BF16)
    c0 = 0
    for o_ref, width in zip(o_refs, splits):
        for j in range(0, width, 2 * LANES):
            wj = min(2 * LANES, width - j)
            o_ref[:, j:j + wj] = jnp.dot(h, w_ref[:, c0 + j:c0 + j + wj],
                                         preferred_element_type=F32).astype(o_ref.dtype)
        c0 += width


def _norm_proj(x, g, w, splits, dtypes):
    n, d = x.shape
    tm = ROW_BLOCK
    return pl.pallas_call(
        functools.partial(_norm_proj_kernel, splits=splits),
        out_shape=[jax.ShapeDtypeStruct((n, s), dt) for s, dt in zip(splits, dtypes)],
        grid=(n // tm,),
        in_specs=[pl.BlockSpec((tm, d), lambda i: (i, 0)),
                  _const_spec((1, d)),
                  _const_spec(w.shape)],
        out_specs=[pl.BlockSpec((tm, s), lambda i: (i, 0)) for s in splits],
        compiler_params=_params("parallel"),
        name="norm_proj",
    )(x, g, w)


def _mem_kv_kernel(mem_ref, g_ref, w_ref, kg_ref, k_ref, v_ref):
    h = _rms(mem_ref[0], g_ref[0]).astype(BF16)
    kv = jnp.dot(h, w_ref[0], preferred_element_type=F32)
    k_ref[0, 0] = _head_rms(kv[:, :D_MEMH], kg_ref[0]).astype(BF16)
    v_ref[0, 0] = kv[:, D_MEMH:].astype(BF16)


def _mem_kv(mem, mem_norm, w_mem_kv, mem_k_gain):
    b, m, d = mem.shape
    depth = w_mem_kv.shape[0]
    out = jax.ShapeDtypeStruct((depth, b, m, D_MEMH), BF16)
    return pl.pallas_call(
        _mem_kv_kernel,
        out_shape=[out, out],
        grid=(depth, b),
        in_specs=[pl.BlockSpec((1, m, d), lambda l, i: (i, 0, 0)),
                  pl.BlockSpec((1, 1, d), lambda l, i: (l, 0, 0)),
                  pl.BlockSpec((1, d, 2 * D_MEMH), lambda l, i: (l, 0, 0)),
                  pl.BlockSpec((1, 1, D_MEMH), lambda l, i: (l, 0, 0))],
        out_specs=[pl.BlockSpec((1, 1, m, D_MEMH), lambda l, i: (l, i, 0, 0))] * 2,
        compiler_params=_params("parallel", "parallel"),
        name="mem_kv",
    )(mem, mem_norm[:, None, :], w_mem_kv.astype(BF16),
      jnp.tile(mem_k_gain, (1, N_MEM_HEADS))[:, None, :])


def _mem_attn_kernel(q_ref, k_ref, v_ref, qg_ref, o_ref):
    qn = _head_rms(q_ref[0], qg_ref[...]) * ATTN_SCALE
    k = k_ref[0, 0]
    v = v_ref[0, 0]
    lane = lax.broadcasted_iota(jnp.int32, qn.shape, 1)
    out = jnp.zeros(qn.shape, F32)
    for h in range(N_MEM_HEADS):
        sel = jnp.logical_and(lane >= h * HEAD_DIM, lane < (h + 1) * HEAD_DIM)
        s = _dot_nt(jnp.where(sel, qn, 0.0), k)
        e = jnp.exp(s - jnp.max(s, axis=-1, keepdims=True))
        o = _dot(e, v) / jnp.sum(e, axis=-1, keepdims=True)
        out = jnp.where(sel, o, out)
    o_ref[0] = out.astype(o_ref.dtype)


def _mem_attn(q_cols, k_mem, v_mem, q_gain, layer, bsz):
    n, _ = q_cols.shape
    t = n // bsz
    tq = ATTN_BLOCK
    m = k_mem.shape[2]
    kv_spec = pl.BlockSpec((1, 1, m, D_MEMH), lambda b, i: (layer, b, 0, 0))
    out = pl.pallas_call(
        _mem_attn_kernel,
        out_shape=jax.ShapeDtypeStruct((bsz, t, D_MEMH), BF16),
        grid=(bsz, t // tq),
        in_specs=[pl.BlockSpec((1, tq, D_MEMH), lambda b, i: (b, i, 0)),
                  kv_spec, kv_spec, _const_spec((1, D_MEMH))],
        out_specs=pl.BlockSpec((1, tq, D_MEMH), lambda b, i: (b, i, 0)),
        compiler_params=_params("parallel", "parallel"),
        name="mem_attn",
    )(q_cols.reshape(bsz, t, D_MEMH), k_mem, v_mem, jnp.tile(q_gain, N_MEM_HEADS)[None, :])
    return out.reshape(n, D_MEMH)


def _rwkv_kernel(u_ref, mu_ref, wup_ref, aup_ref, gup_ref, w0_ref, a0_ref, kk_ref, ka_ref,
                 rk_ref, lng_ref, lnb_ref, o_ref, state_ref, prev_ref):
    L = u_ref.shape[0]
    L2 = 2 * L
    assert L2 == LANES
    c = pl.program_id(1)

    @pl.when(c == 0)
    def _():
        state_ref[...] = jnp.zeros_like(state_ref)
        prev_ref[...] = jnp.zeros_like(prev_ref)

    u = u_ref[...]
    row = lax.broadcasted_iota(jnp.int32, u.shape, 0)
    shifted = jnp.where(row == 0, prev_ref[...], pltpu.roll(u, 1, axis=0))
    prev_ref[...] = u[L - 1:L, :]
    us = u + (shifted - u) * mu_ref[...]

    r = us[:, 0:D_MAIN]
    k = us[:, D_MAIN:2 * D_MAIN]
    v = us[:, 2 * D_MAIN:3 * D_MAIN]
    x_wa = us[:, 3 * D_MAIN:3 * D_MAIN + D_LORA_WA]
    x_g = us[:, 3 * D_MAIN + D_LORA_WA:]

    w_log = -_softplus(-(w0_ref[...] + _dot(jnp.tanh(x_wa), wup_ref[...]))) - 0.5
    log_w = -jnp.exp(w_log)
    lr = _sigmoid(a0_ref[...] + _dot(x_wa, aup_ref[...]))
    gate = _dot(_sigmoid(x_g), gup_ref[...])
    kk_raw = k * kk_ref[...]
    k = k * (1.0 + (lr - 1.0) * ka_ref[...])

    cum = _dot_f32(_tril(L, strict=False).astype(F32), log_w)
    g_incl = jnp.exp(cum)
    g_excl = jnp.exp(cum - log_w)
    g_inv = jnp.exp(-cum)
    g_tail = jnp.exp(cum[L - 1:L, :] - cum)

    lo = _lo_mask((L, LANES))
    own = (lax.broadcasted_iota(jnp.int32, (L2, LANES), 0) < L) == _lo_mask((L2, LANES))
    tok_r = lax.broadcasted_iota(jnp.int32, (L2, L2), 0) & (L - 1)
    tok_c = lax.broadcasted_iota(jnp.int32, (L2, L2), 1) & (L - 1)
    strict = tok_r > tok_c
    incl = tok_r >= tok_c
    eye = (lax.broadcasted_iota(jnp.int32, (L2, L2), 0)
           == lax.broadcasted_iota(jnp.int32, (L2, L2), 1)).astype(F32)

    def stack(x):
        return jnp.where(own, jnp.concatenate([x, x], axis=0), 0.0)

    pairs = range(N_PAIRS)
    sls = [slice(p * LANES, (p + 1) * LANES) for p in pairs]
    kkn = []
    for sl in sls:
        kk = kk_raw[:, sl]
        kkn.append(kk * lax.rsqrt(jnp.maximum(_pair_sum(kk * kk, lo), 1e-24)))
    b = [kkn[p] * lr[:, sls[p]] for p in pairs]
    ar = [jnp.concatenate([stack(-kkn[p] * g_excl[:, sls[p]]),
                           stack(r[:, sls[p]] * g_incl[:, sls[p]])], axis=0).astype(BF16)
          for p in pairs]
    v2 = [stack(v[:, sl]).astype(BF16) for sl in sls]
    states = [state_ref[p] for p in pairs]
    rhs = [jnp.concatenate([stack(b[p] * g_inv[:, sls[p]]).astype(BF16),
                            stack(k[:, sls[p]] * g_inv[:, sls[p]]).astype(BF16),
                            states[p].astype(BF16)], axis=0) for p in pairs]
    g = [_dot_nt(ar[p], rhs[p]) for p in pairs]
    power = [jnp.where(strict, g[p][:L2, :L2], 0.0) for p in pairs]
    a_ak = [jnp.where(strict, g[p][:L2, L2:2 * L2], 0.0).astype(BF16) for p in pairs]
    a_r = [jnp.concatenate([jnp.where(incl, g[p][L2:, :L2], 0.0),
                            jnp.where(incl, g[p][L2:, L2:2 * L2], 0.0)], axis=1).astype(BF16)
           for p in pairs]

    inv = [eye + n for n in power]
    for _ in range(L.bit_length() - 2):
        power = [_dot(n, n) for n in power]
        inv = [t + _dot(n, t) for n, t in zip(power, inv)]

    w_mat = [g[p][:L2, 2 * L2:] + _dot(a_ak[p], v2[p]) for p in pairs]
    uv = [jnp.concatenate([_dot(inv[p], w_mat[p]).astype(BF16), v2[p]], axis=0) for p in pairs]
    y2 = [g[p][L2:, 2 * L2:] + _dot(a_r[p], uv[p]) for p in pairs]
    for p in pairs:
        sl = sls[p]
        bk = jnp.concatenate([stack(b[p] * g_tail[:, sl]), stack(k[:, sl] * g_tail[:, sl])], axis=0)
        state_ref[p] = states[p] * g_incl[L - 1:L, sl] + _dot_tn(uv[p], bk)

    for p in pairs:
        sl = sls[p]
        y = y2[p][:L] + y2[p][L:]
        mean = _pair_sum(y, lo) * (1.0 / HEAD_DIM)
        yc = y - mean
        var = _pair_sum(yc * yc, lo) * (1.0 / HEAD_DIM)
        yn = yc * lax.rsqrt(var + GN_EPS) * lng_ref[:, sl] + lnb_ref[:, sl]
        bonus = _pair_sum(r[:, sl] * k[:, sl] * rk_ref[:, sl], lo) * v[:, sl]
        o_ref[:, sl] = ((yn + bonus) * gate[:, sl]).astype(o_ref.dtype)


def _rwkv(u_main, bsz, mu, w_up, a_up, g_up, w0, a0, k_k, k_a, r_k, lnx_g, lnx_b):
    n, width = u_main.shape
    nc = n // bsz // RWKV_CHUNK
    rows = [w0, a0, k_k, k_a, r_k.reshape(-1), lnx_g, lnx_b]
    zeros_wa = jnp.zeros((D_DECAY_LORA, D_MAIN), F32)
    wup = jnp.concatenate([w_up, zeros_wa], axis=0)
    aup = jnp.concatenate([zeros_wa, a_up], axis=0)
    gup = jnp.concatenate([g_up, jnp.zeros((D_LORA_G - D_GATE_LORA, D_MAIN), F32)], axis=0)
    return pl.pallas_call(
        _rwkv_kernel,
        out_shape=jax.ShapeDtypeStruct((n, D_MAIN), BF16),
        grid=(bsz, nc),
        in_specs=[pl.BlockSpec((RWKV_CHUNK, width), lambda b, c: (b * nc + c, 0)),
                  _const_spec((1, width)),
                  _const_spec(wup.shape), _const_spec(aup.shape), _const_spec(gup.shape)]
                 + [_const_spec((1, D_MAIN))] * len(rows),
        out_specs=pl.BlockSpec((RWKV_CHUNK, D_MAIN), lambda b, c: (b * nc + c, 0)),
        scratch_shapes=[pltpu.VMEM((N_PAIRS, LANES, LANES), F32),
                        pltpu.VMEM((1, width), F32)],
        compiler_params=_params("parallel", "arbitrary"),
        name="rwkv7",
    )(u_main, mu[None, :], wup, aup, gup, *[x[None, :] for x in rows])


def _split3(x):
    hi = x.astype(BF16).astype(F32)
    mid = (x - hi).astype(BF16).astype(F32)
    lo = (x - hi - mid).astype(BF16).astype(F32)
    return hi, mid, lo


def _bias_lanes(own, lane, base, ones_first, terms):
    one_at = base if ones_first else base + 3
    term_at = base + 3 if ones_first else base
    out = jnp.where(jnp.logical_and(lane >= one_at, lane < one_at + 3), 1.0, 0.0)
    for i, term in enumerate(terms):
        out = jnp.where(lane == term_at + i, term, out)
    return jnp.where(own, 0.0, out)


def _shared_kv_kernel(x_ref, g_ref, w_ref, bf_ref, kg_ref, k_ref, v_ref, c_ref, carry_ref):
    @pl.when(pl.program_id(1) == 0)
    def _():
        carry_ref[...] = jnp.zeros_like(carry_ref)

    tm = x_ref.shape[0]
    h = _rms(x_ref[...], g_ref[...]).astype(BF16)
    logits = jnp.dot(h, w_ref[:, 2 * D_MAIN:], preferred_element_type=F32) + bf_ref[...]
    log_f = -_softplus(-logits)
    tri = _tril(CUMSUM_BLOCK, strict=False).astype(F32)
    carry = carry_ref[...]
    for j in range(0, tm, CUMSUM_BLOCK):
        cj = _dot_f32(tri, log_f[j:j + CUMSUM_BLOCK]) + carry
        c_ref[j:j + CUMSUM_BLOCK, :] = cj * LOG2E
        carry = cj[CUMSUM_BLOCK - 1:CUMSUM_BLOCK, :]
    carry_ref[...] = carry

    c2 = c_ref[...]
    lane = lax.broadcasted_iota(jnp.int32, (tm, LANES), 1)
    lo = lane < HEAD_DIM
    for p in range(N_PAIRS):
        sl = slice(p * LANES, (p + 1) * LANES)
        kn = _head_rms(jnp.dot(h, w_ref[:, sl], preferred_element_type=F32), kg_ref[:, sl])
        v_ref[:, sl] = jnp.dot(h, w_ref[:, D_MAIN + p * LANES:D_MAIN + (p + 1) * LANES],
                               preferred_element_type=F32).astype(BF16)
        for j in range(2):
            head = 2 * p + j
            own = lo if j == 0 else jnp.logical_not(lo)
            ck = jnp.sum(jnp.where(lane == head, c2, 0.0), axis=-1, keepdims=True)
            bias = _bias_lanes(own, lane, HEAD_DIM * (1 - j), False, [-t for t in _split3(ck)])
            k_ref[:, head * LANES:(head + 1) * LANES] = jnp.where(own, kn, bias).astype(BF16)


def _shared_kv(x, bsz, kv_norm, w_kv, b_f, k_gain):
    n, d = x.shape
    tm = ROW_BLOCK
    nt = n // bsz // tm
    pad = LANES - N_MAIN_HEADS
    w = jnp.pad(w_kv, ((0, 0), (0, pad))).astype(BF16)
    row = lambda b, i: (b * nt + i, 0)
    return pl.pallas_call(
        _shared_kv_kernel,
        out_shape=[jax.ShapeDtypeStruct((n, N_MAIN_HEADS * LANES), BF16),
                   jax.ShapeDtypeStruct((n, D_MAIN), BF16),
                   jax.ShapeDtypeStruct((n, LANES), F32)],
        grid=(bsz, nt),
        in_specs=[pl.BlockSpec((tm, d), row), _const_spec((1, d)), _const_spec(w.shape),
                  _const_spec((1, LANES)), _const_spec((1, D_MAIN))],
        out_specs=[pl.BlockSpec((tm, N_MAIN_HEADS * LANES), row), pl.BlockSpec((tm, D_MAIN), row),
                   pl.BlockSpec((tm, LANES), row)],
        scratch_shapes=[pltpu.VMEM((1, LANES), F32)],
        compiler_params=_params("parallel", "arbitrary"),
        name="shared_kv",
    )(x, kv_norm[None, :], w, jnp.pad(b_f, (0, pad))[None, :],
      jnp.tile(k_gain, N_MAIN_HEADS)[None, :])


def _fox_kernel(q_ref, k0_ref, k1_ref, v_ref, c_ref, gate_ref, qg_ref, o_ref):
    tq = q_ref.shape[1]
    p = pl.program_id(1)
    qi = pl.program_id(2)
    qn = _head_rms(q_ref[0], qg_ref[...]) * (ATTN_SCALE * LOG2E)
    lane = lax.broadcasted_iota(jnp.int32, qn.shape, 1)
    lo = lane < HEAD_DIM
    c2 = c_ref[0]
    causal = _tril(tq, strict=False)
    k_refs = (k0_ref, k1_ref)
    q_tiles = []
    for j in range(2):
        own = lo if j == 0 else jnp.logical_not(lo)
        cq = jnp.sum(jnp.where(lane == 2 * p + j, c2, 0.0), axis=-1, keepdims=True)
        bias = _bias_lanes(own, lane, HEAD_DIM * (1 - j), True, _split3(cq))
        q_tiles.append(jnp.where(own, qn, bias).astype(BF16))

    def step(start, carry, masked):
        vb = v_ref[0, pl.ds(start, tq), :]
        out = []
        for j in range(2):
            m, l, acc = carry[j]
            s = _dot_nt(q_tiles[j], k_refs[j][0, pl.ds(start, tq), :])
            if masked:
                s = jnp.where(causal, s, NEG_INF)
            m_new = jnp.maximum(m, jnp.max(s, axis=-1, keepdims=True))
            alpha = jnp.exp2(m - m_new)
            e = jnp.exp2(s - m_new)
            l = alpha * l + jnp.sum(e, axis=-1, keepdims=True)
            acc = alpha * acc + _dot(e, vb)
            out.append((m_new, l, acc))
        return tuple(out)

    init = (jnp.full((tq, 1), NEG_INF, F32), jnp.zeros((tq, 1), F32), jnp.zeros((tq, LANES), F32))
    carry = lax.fori_loop(
        0, qi, lambda kb, carry: step(pl.multiple_of(kb * tq, tq), carry, False), (init, init))
    (_, l0, acc0), (_, l1, acc1) = step(pl.multiple_of(qi * tq, tq), carry, True)
    o = jnp.where(lo, acc0 / l0, acc1 / l1)
    o_ref[0] = (o * _sigmoid(gate_ref[0])).astype(o_ref.dtype)


def _fox(q, gate, k_tiles, v, c2, q_gain, bsz):
    n, _ = q.shape
    t = n // bsz
    tq = ATTN_BLOCK
    blk = pl.BlockSpec((1, tq, LANES), lambda b, p, i: (b, i, p))
    out = pl.pallas_call(
        _fox_kernel,
        out_shape=jax.ShapeDtypeStruct((bsz, t, D_MAIN), BF16),
        grid=(bsz, N_PAIRS, t // tq),
        in_specs=[blk,
                  pl.BlockSpec((1, t, LANES), lambda b, p, i: (b, 0, 2 * p)),
                  pl.BlockSpec((1, t, LANES), lambda b, p, i: (b, 0, 2 * p + 1)),
                  pl.BlockSpec((1, t, LANES), lambda b, p, i: (b, 0, p)),
                  pl.BlockSpec((1, tq, LANES), lambda b, p, i: (b, i, 0)),
                  blk, _const_spec((1, LANES))],
        out_specs=blk,
        compiler_params=_params("parallel", "parallel", "arbitrary"),
        name="fox_attn",
    )(q.reshape(bsz, t, D_MAIN), k_tiles.reshape(bsz, t, N_MAIN_HEADS * LANES),
      k_tiles.reshape(bsz, t, N_MAIN_HEADS * LANES), v.reshape(bsz, t, D_MAIN),
      c2.reshape(bsz, t, LANES), gate.reshape(bsz, t, D_MAIN), jnp.tile(q_gain, 2)[None, :])
    return out.reshape(n, D_MAIN)


def _merge_ffn_kernel(x_ref, main_ref, memo_ref, wo_ref, g_ref, wgu_ref, wd_ref, o_ref):
    d_ff = wd_ref.shape[0]
    mixed = (jnp.dot(main_ref[...], wo_ref[:D_MAIN, :], preferred_element_type=F32)
             + jnp.dot(memo_ref[...], wo_ref[D_MAIN:, :], preferred_element_type=F32))
    x = x_ref[...] + mixed
    h = _rms(x, g_ref[...]).astype(BF16)
    acc = x
    for j in range(0, d_ff, FF_CHUNK):
        gj = jnp.dot(h, wgu_ref[:, j:j + FF_CHUNK], preferred_element_type=F32)
        uj = jnp.dot(h, wgu_ref[:, d_ff + j:d_ff + j + FF_CHUNK], preferred_element_type=F32)
        act = (gj * _sigmoid(gj) * uj).astype(BF16)
        acc = acc + jnp.dot(act, wd_ref[j:j + FF_CHUNK, :], preferred_element_type=F32)
    o_ref[...] = acc


def _merge_ffn(x, main, memo, w_out, ffn_norm, w_gate_up, w_down):
    n, d = x.shape
    tm = ROW_BLOCK
    row = lambda i: (i, 0)
    single = pl.Buffered(1)
    return pl.pallas_call(
        _merge_ffn_kernel,
        out_shape=jax.ShapeDtypeStruct((n, d), F32),
        grid=(n // tm,),
        in_specs=[pl.BlockSpec((tm, d), row), pl.BlockSpec((tm, D_MAIN), row),
                  pl.BlockSpec((tm, D_MEMH), row),
                  pl.BlockSpec(w_out.shape, lambda i: (0, 0), pipeline_mode=single),
                  _const_spec((1, d)),
                  pl.BlockSpec(w_gate_up.shape, lambda i: (0, 0), pipeline_mode=single),
                  pl.BlockSpec(w_down.shape, lambda i: (0, 0), pipeline_mode=single)],
        out_specs=pl.BlockSpec((tm, d), row),
        compiler_params=_params("parallel"),
        name="merge_ffn",
    )(x, main, memo, w_out.astype(BF16), ffn_norm[None, :], w_gate_up.astype(BF16),
      w_down.astype(BF16))


def _a_in_weights(w_in, mu):
    d_shift = 3 * D_MAIN + D_DECAY_LORA + D_AAA_LORA + D_GATE_LORA
    pad = D_LORA_G - D_GATE_LORA
    w = jnp.concatenate([w_in[:, :d_shift], jnp.zeros((w_in.shape[0], pad), w_in.dtype),
                         w_in[:, d_shift:]], axis=1)
    return w.astype(BF16), jnp.pad(mu, (0, pad))


def kernel(x, mem, mix_norm, w_out, mem_norm, w_mem_kv, mem_q_gain, mem_k_gain, ffn_norm, w_gate_up, w_down, a_w_in, a_mu, a_w0, a_w_up, a_a0, a_a_up, a_g_up, a_k_k, a_k_a, a_r_k, a_lnx_g, a_lnx_b, kv_norm, w_kv, b_f, k_gain, b_w_in, b_q_gain):
    bsz, t, d = x.shape
    n_a = a_w_in.shape[0]
    n_b = b_w_in.shape[0]
    x = x.reshape(bsz * t, d)
    k_mem, v_mem = _mem_kv(mem, mem_norm, w_mem_kv, mem_k_gain)

    def merge(x, l, main, mem_q):
        memo = _mem_attn(mem_q, k_mem, v_mem, mem_q_gain[l], l, bsz)
        return _merge_ffn(x, main, memo, w_out[l], ffn_norm[l], w_gate_up[l], w_down[l])

    for i in range(n_a):
        w, mu = _a_in_weights(a_w_in[i], a_mu[i])
        u_main, mem_q = _norm_proj(x, mix_norm[i][None, :], w, (D_A_MAIN, D_MEMH), (F32, F32))
        main = _rwkv(u_main, bsz, mu, a_w_up[i], a_a_up[i], a_g_up[i], a_w0[i], a_a0[i],
                     a_k_k[i], a_k_a[i], a_r_k[i], a_lnx_g[i], a_lnx_b[i])
        x = merge(x, i, main, mem_q)

    k_sh, v_sh, c_sh = _shared_kv(x, bsz, kv_norm, w_kv, b_f, k_gain)

    for j in range(n_b):
        l = n_a + j
        q, gate, mem_q = _norm_proj(x, mix_norm[l][None, :], b_w_in[j].astype(BF16),
                                    (D_MAIN, D_MAIN, D_MEMH), (F32, F32, F32))
        main = _fox(q, gate, k_sh, v_sh, c_sh, b_q_gain[j], bsz)
        x = merge(x, l, main, mem_q)

    return x.reshape(bsz, t, d)
```

```python
import functools

import jax
import jax.numpy as jnp
from jax import lax
from jax.experimental import pallas as pl
from jax.experimental.pallas import tpu as pltpu

F32 = jnp.float32
BF16 = jnp.bfloat16
HIGHEST = lax.Precision.HIGHEST

HEAD_DIM = 64
LANES = 128
N_MAIN_HEADS = 12
N_PAIRS = N_MAIN_HEADS // 2
D_MAIN = N_MAIN_HEADS * HEAD_DIM
N_MEM_HEADS = 4
D_MEMH = N_MEM_HEADS * HEAD_DIM
D_DECAY_LORA = 64
D_AAA_LORA = 64
D_GATE_LORA = 160
D_LORA_WA = LANES
D_LORA_G = 2 * LANES
D_A_MAIN = 3 * D_MAIN + D_LORA_WA + D_LORA_G
RMS_EPS = 1e-6
GN_EPS = 64e-5
ATTN_SCALE = HEAD_DIM ** -0.5
NEG_INF = -1e30
LOG2E = 1.4426950408889634
VMEM_LIMIT = 56 * 1024 * 1024

RWKV_CHUNK = 64
ROW_BLOCK = 512
ATTN_BLOCK = 512
CUMSUM_BLOCK = 128
FF_CHUNK = 1408


def _params(*sem):
    return pltpu.CompilerParams(dimension_semantics=sem, vmem_limit_bytes=VMEM_LIMIT)


def _const_spec(shape):
    nd = len(shape)
    return pl.BlockSpec(shape, lambda *_: (0,) * nd)


def _dot(a, b):
    return jnp.dot(a.astype(BF16), b.astype(BF16), preferred_element_type=F32)


def _dot_nt(a, b):
    return lax.dot_general(a.astype(BF16), b.astype(BF16), (((1,), (1,)), ((), ())),
                           preferred_element_type=F32)


def _dot_tn(a, b):
    return lax.dot_general(a.astype(BF16), b.astype(BF16), (((0,), (0,)), ((), ())),
                           preferred_element_type=F32)


def _dot_f32(a, b):
    return jnp.dot(a, b, preferred_element_type=F32, precision=HIGHEST)


def _sigmoid(x):
    return 1.0 / (1.0 + jnp.exp(-x))


def _softplus(x):
    return jnp.maximum(x, 0.0) + jnp.log(1.0 + jnp.exp(-jnp.abs(x)))


def _rms(x, g):
    return x * lax.rsqrt(jnp.mean(x * x, axis=-1, keepdims=True) + RMS_EPS) * g


def _lo_mask(shape):
    return lax.broadcasted_iota(jnp.int32, shape, len(shape) - 1) < HEAD_DIM


def _pair_sum(x, lo):
    s_lo = jnp.sum(jnp.where(lo, x, 0.0), axis=-1, keepdims=True)
    s_hi = jnp.sum(jnp.where(lo, 0.0, x), axis=-1, keepdims=True)
    return jnp.where(lo, s_lo, s_hi)


def _head_rms(x, gain):
    outs = []
    for j in range(x.shape[-1] // LANES):
        xb = x[:, j * LANES:(j + 1) * LANES]
        lo = _lo_mask(xb.shape)
        ms = _pair_sum(xb * xb, lo) * (1.0 / HEAD_DIM)
        outs.append(xb * lax.rsqrt(ms + RMS_EPS))
    y = outs[0] if len(outs) == 1 else jnp.concatenate(outs, axis=-1)
    return y * gain


def _tril(n, strict):
    r = lax.broadcasted_iota(jnp.int32, (n, n), 0)
    c = lax.broadcasted_iota(jnp.int32, (n, n), 1)
    return (r > c) if strict else (r >= c)


def _norm_proj_kernel(x_ref, g_ref, w_ref, *o_refs, splits):
    h = _rms(x_ref[...], g_ref[...]).astype(BF16)
    c0 = 0
    for o_ref, width in zip(o_refs, splits):
        for j in range(0, width, 2 * LANES):
            wj = min(2 * LANES, width - j)
            o_ref[:, j:j + wj] = jnp.dot(h, w_ref[:, c0 + j:c0 + j + wj],
                                         preferred_element_type=F32).astype(o_ref.dtype)
        c0 += width


def _norm_proj(x, g, w, splits, dtypes):
    n, d = x.shape
    tm = ROW_BLOCK
    return pl.pallas_call(
        functools.partial(_norm_proj_kernel, splits=splits),
        out_shape=[jax.ShapeDtypeStruct((n, s), dt) for s, dt in zip(splits, dtypes)],
        grid=(n // tm,),
        in_specs=[pl.BlockSpec((tm, d), lambda i: (i, 0)),
                  _const_spec((1, d)),
                  _const_spec(w.shape)],
        out_specs=[pl.BlockSpec((tm, s), lambda i: (i, 0)) for s in splits],
        compiler_params=_params("parallel"),
        name="norm_proj",
    )(x, g, w)


def _mem_kv_kernel(mem_ref, g_ref, w_ref, kg_ref, k_ref, v_ref):
    h = _rms(mem_ref[0], g_ref[0]).astype(BF16)
    kv = jnp.dot(h, w_ref[0], preferred_element_type=F32)
    k_ref[0, 0] = _head_rms(kv[:, :D_MEMH], kg_ref[0]).astype(BF16)
    v_ref[0, 0] = kv[:, D_MEMH:].astype(BF16)


def _mem_kv(mem, mem_norm, w_mem_kv, mem_k_gain):
    b, m, d = mem.shape
    depth = w_mem_kv.shape[0]
    out = jax.ShapeDtypeStruct((depth, b, m, D_MEMH), BF16)
    return pl.pallas_call(
        _mem_kv_kernel,
        out_shape=[out, out],
        grid=(depth, b),
        in_specs=[pl.BlockSpec((1, m, d), lambda l, i: (i, 0, 0)),
                  pl.BlockSpec((1, 1, d), lambda l, i: (l, 0, 0)),
                  pl.BlockSpec((1, d, 2 * D_MEMH), lambda l, i: (l, 0, 0)),
                  pl.BlockSpec((1, 1, D_MEMH), lambda l, i: (l, 0, 0))],
        out_specs=[pl.BlockSpec((1, 1, m, D_MEMH), lambda l, i: (l, i, 0, 0))] * 2,
        compiler_params=_params("parallel", "parallel"),
        name="mem_kv",
    )(mem, mem_norm[:, None, :], w_mem_kv.astype(BF16),
      jnp.tile(mem_k_gain, (1, N_MEM_HEADS))[:, None, :])


def _mem_attn_kernel(q_ref, k_ref, v_ref, qg_ref, o_ref):
    qn = _head_rms(q_ref[0], qg_ref[...]) * ATTN_SCALE
    k = k_ref[0, 0]
    v = v_ref[0, 0]
    lane = lax.broadcasted_iota(jnp.int32, qn.shape, 1)
    out = jnp.zeros(qn.shape, F32)
    for h in range(N_MEM_HEADS):
        sel = jnp.logical_and(lane >= h * HEAD_DIM, lane < (h + 1) * HEAD_DIM)
        s = _dot_nt(jnp.where(sel, qn, 0.0), k)
        e = jnp.exp(s - jnp.max(s, axis=-1, keepdims=True))
        o = _dot(e, v) / jnp.sum(e, axis=-1, keepdims=True)
        out = jnp.where(sel, o, out)
    o_ref[0] = out.astype(o_ref.dtype)


def _mem_attn(q_cols, k_mem, v_mem, q_gain, layer, bsz):
    n, _ = q_cols.shape
    t = n // bsz
    tq = ATTN_BLOCK
    m = k_mem.shape[2]
    kv_spec = pl.BlockSpec((1, 1, m, D_MEMH), lambda b, i: (layer, b, 0, 0))
    out = pl.pallas_call(
        _mem_attn_kernel,
        out_shape=jax.ShapeDtypeStruct((bsz, t, D_MEMH), BF16),
        grid=(bsz, t // tq),
        in_specs=[pl.BlockSpec((1, tq, D_MEMH), lambda b, i: (b, i, 0)),
                  kv_spec, kv_spec, _const_spec((1, D_MEMH))],
        out_specs=pl.BlockSpec((1, tq, D_MEMH), lambda b, i: (b, i, 0)),
        compiler_params=_params("parallel", "parallel"),
        name="mem_attn",
    )(q_cols.reshape(bsz, t, D_MEMH), k_mem, v_mem, jnp.tile(q_gain, N_MEM_HEADS)[None, :])
    return out.reshape(n, D_MEMH)


def _rwkv_kernel(u_ref, mu_ref, wup_ref, aup_ref, gup_ref, w0_ref, a0_ref, kk_ref, ka_ref,
                 rk_ref, lng_ref, lnb_ref, o_ref, state_ref, prev_ref):
    L = u_ref.shape[0]
    L2 = 2 * L
    assert L2 == LANES
    c = pl.program_id(1)

    @pl.when(c == 0)
    def _():
        state_ref[...] = jnp.zeros_like(state_ref)
        prev_ref[...] = jnp.zeros_like(prev_ref)

    u = u_ref[...]
    row = lax.broadcasted_iota(jnp.int32, u.shape, 0)
    shifted = jnp.where(row == 0, prev_ref[...], pltpu.roll(u, 1, axis=0))
    prev_ref[...] = u[L - 1:L, :]
    us = u + (shifted - u) * mu_ref[...]

    r = us[:, 0:D_MAIN]
    k = us[:, D_MAIN:2 * D_MAIN]
    v = us[:, 2 * D_MAIN:3 * D_MAIN]
    x_wa = us[:, 3 * D_MAIN:3 * D_MAIN + D_LORA_WA]
    x_g = us[:, 3 * D_MAIN + D_LORA_WA:]

    w_log = -_softplus(-(w0_ref[...] + _dot(jnp.tanh(x_wa), wup_ref[...]))) - 0.5
    log_w = -jnp.exp(w_log)
    lr = _sigmoid(a0_ref[...] + _dot(x_wa, aup_ref[...]))
    gate = _dot(_sigmoid(x_g), gup_ref[...])
    kk_raw = k * kk_ref[...]
    k = k * (1.0 + (lr - 1.0) * ka_ref[...])

    cum = _dot_f32(_tril(L, strict=False).astype(F32), log_w)
    g_incl = jnp.exp(cum)
    g_excl = jnp.exp(cum - log_w)
    g_inv = jnp.exp(-cum)
    g_tail = jnp.exp(cum[L - 1:L, :] - cum)

    lo = _lo_mask((L, LANES))
    own = (lax.broadcasted_iota(jnp.int32, (L2, LANES), 0) < L) == _lo_mask((L2, LANES))
    tok_r = lax.broadcasted_iota(jnp.int32, (L2, L2), 0) & (L - 1)
    tok_c = lax.broadcasted_iota(jnp.int32, (L2, L2), 1) & (L - 1)
    strict = tok_r > tok_c
    incl = tok_r >= tok_c
    eye = (lax.broadcasted_iota(jnp.int32, (L2, L2), 0)
           == lax.broadcasted_iota(jnp.int32, (L2, L2), 1)).astype(F32)

    def stack(x):
        return jnp.where(own, jnp.concatenate([x, x], axis=0), 0.0)

    pairs = range(N_PAIRS)
    sls = [slice(p * LANES, (p + 1) * LANES) for p in pairs]
    kkn = []
    for sl in sls:
        kk = kk_raw[:, sl]
        kkn.append(kk * lax.rsqrt(jnp.maximum(_pair_sum(kk * kk, lo), 1e-24)))
    b = [kkn[p] * lr[:, sls[p]] for p in pairs]
    ar = [jnp.concatenate([stack(-kkn[p] * g_excl[:, sls[p]]),
                           stack(r[:, sls[p]] * g_incl[:, sls[p]])], axis=0).astype(BF16)
          for p in pairs]
    v2 = [stack(v[:, sl]).astype(BF16) for sl in sls]
    states = [state_ref[p] for p in pairs]
    rhs = [jnp.concatenate([stack(b[p] * g_inv[:, sls[p]]).astype(BF16),
                            stack(k[:, sls[p]] * g_inv[:, sls[p]]).astype(BF16),
                            states[p].astype(BF16)], axis=0) for p in pairs]
    g = [_dot_nt(ar[p], rhs[p]) for p in pairs]
    power = [jnp.where(strict, g[p][:L2, :L2], 0.0) for p in pairs]
    a_ak = [jnp.where(strict, g[p][:L2, L2:2 * L2], 0.0).astype(BF16) for p in pairs]
    a_r = [jnp.concatenate([jnp.where(incl, g[p][L2:, :L2], 0.0),
                            jnp.where(incl, g[p][L2:, L2:2 * L2], 0.0)], axis=1).astype(BF16)
           for p in pairs]

    inv = [eye + n for n in power]
    for _ in range(L.bit_length() - 2):
        power = [_dot(n, n) for n in power]
        inv = [t + _dot(n, t) for n, t in zip(power, inv)]

    w_mat = [g[p][:L2, 2 * L2:] + _dot(a_ak[p], v2[p]) for p in pairs]
    uv = [jnp.concatenate([_dot(inv[p], w_mat[p]).astype(BF16), v2[p]], axis=0) for p in pairs]
    y2 = [g[p][L2:, 2 * L2:] + _dot(a_r[p], uv[p]) for p in pairs]
    for p in pairs:
        sl = sls[p]
        bk = jnp.concatenate([stack(b[p] * g_tail[:, sl]), stack(k[:, sl] * g_tail[:, sl])], axis=0)
        state_ref[p] = states[p] * g_incl[L - 1:L, sl] + _dot_tn(uv[p], bk)

    for p in pairs:
        sl = sls[p]
        y = y2[p][:L] + y2[p][L:]
        mean = _pair_sum(y, lo) * (1.0 / HEAD_DIM)
        yc = y - mean
        var = _pair_sum(yc * yc, lo) * (1.0 / HEAD_DIM)
        yn = yc * lax.rsqrt(var + GN_EPS) * lng_ref[:, sl] + lnb_ref[:, sl]
        bonus = _pair_sum(r[:, sl] * k[:, sl] * rk_ref[:, sl], lo) * v[:, sl]
        o_ref[:, sl] = ((yn + bonus) * gate[:, sl]).astype(o_ref.dtype)


def _rwkv(u_main, bsz, mu, w_up, a_up, g_up, w0, a0, k_k, k_a, r_k, lnx_g, lnx_b):
    n, width = u_main.shape
    nc = n // bsz // RWKV_CHUNK
    rows = [w0, a0, k_k, k_a, r_k.reshape(-1), lnx_g, lnx_b]
    zeros_wa = jnp.zeros((D_DECAY_LORA, D_MAIN), F32)
    wup = jnp.concatenate([w_up, zeros_wa], axis=0)
    aup = jnp.concatenate([zeros_wa, a_up], axis=0)
    gup = jnp.concatenate([g_up, jnp.zeros((D_LORA_G - D_GATE_LORA, D_MAIN), F32)], axis=0)
    return pl.pallas_call(
        _rwkv_kernel,
        out_shape=jax.ShapeDtypeStruct((n, D_MAIN), BF16),
        grid=(bsz, nc),
        in_specs=[pl.BlockSpec((RWKV_CHUNK, width), lambda b, c: (b * nc + c, 0)),
                  _const_spec((1, width)),
                  _const_spec(wup.shape), _const_spec(aup.shape), _const_spec(gup.shape)]
                 + [_const_spec((1, D_MAIN))] * len(rows),
        out_specs=pl.BlockSpec((RWKV_CHUNK, D_MAIN), lambda b, c: (b * nc + c, 0)),
        scratch_shapes=[pltpu.VMEM((N_PAIRS, LANES, LANES), F32),
                        pltpu.VMEM((1, width), F32)],
        compiler_params=_params("parallel", "arbitrary"),
        name="rwkv7",
    )(u_main, mu[None, :], wup, aup, gup, *[x[None, :] for x in rows])


def _split3(x):
    hi = x.astype(BF16).astype(F32)
    mid = (x - hi).astype(BF16).astype(F32)
    lo = (x - hi - mid).astype(BF16).astype(F32)
    return hi, mid, lo


def _bias_lanes(own, lane, base, ones_first, terms):
    one_at = base if ones_first else base + 3
    term_at = base + 3 if ones_first else base
    out = jnp.where(jnp.logical_and(lane >= one_at, lane < one_at + 3), 1.0, 0.0)
    for i, term in enumerate(terms):
        out = jnp.where(lane == term_at + i, term, out)
    return jnp.where(own, 0.0, out)


def _shared_kv_kernel(x_ref, g_ref, w_ref, wvt_ref, bf_ref, kg_ref, k_ref, vt_ref, c_ref,
                      carry_ref):
    @pl.when(pl.program_id(1) == 0)
    def _():
        carry_ref[...] = jnp.zeros_like(carry_ref)

    tm = x_ref.shape[0]
    h = _rms(x_ref[...], g_ref[...]).astype(BF16)
    vt_ref[0] = _dot_nt(wvt_ref[...], h).astype(BF16)
    logits = jnp.dot(h, w_ref[:, D_MAIN:], preferred_element_type=F32) + bf_ref[...]
    log_f = -_softplus(-logits)
    tri = _tril(CUMSUM_BLOCK, strict=False).astype(F32)
    carry = carry_ref[...]
    for j in range(0, tm, CUMSUM_BLOCK):
        cj = _dot_f32(tri, log_f[j:j + CUMSUM_BLOCK]) + carry
        c_ref[j:j + CUMSUM_BLOCK, :] = cj * LOG2E
        carry = cj[CUMSUM_BLOCK - 1:CUMSUM_BLOCK, :]
    carry_ref[...] = carry

    c2 = c_ref[...]
    lane = lax.broadcasted_iota(jnp.int32, (tm, LANES), 1)
    lo = lane < HEAD_DIM
    for p in range(N_PAIRS):
        sl = slice(p * LANES, (p + 1) * LANES)
        kn = _head_rms(jnp.dot(h, w_ref[:, sl], preferred_element_type=F32), kg_ref[:, sl])
        for j in range(2):
            head = 2 * p + j
            own = lo if j == 0 else jnp.logical_not(lo)
            ck = jnp.sum(jnp.where(lane == head, c2, 0.0), axis=-1, keepdims=True)
            bias = _bias_lanes(own, lane, HEAD_DIM * (1 - j), False, [-t for t in _split3(ck)])
            k_ref[:, head * LANES:(head + 1) * LANES] = jnp.where(own, kn, bias).astype(BF16)


def _shared_kv(x, bsz, kv_norm, w_kv, b_f, k_gain):
    n, d = x.shape
    tm = ROW_BLOCK
    nt = n // bsz // tm
    pad = LANES - N_MAIN_HEADS
    w = jnp.concatenate([w_kv[:, :D_MAIN], jnp.pad(w_kv[:, 2 * D_MAIN:], ((0, 0), (0, pad)))],
                        axis=1).astype(BF16)
    w_vt = w_kv[:, D_MAIN:2 * D_MAIN].T.astype(BF16)
    row = lambda b, i: (b * nt + i, 0)
    return pl.pallas_call(
        _shared_kv_kernel,
        out_shape=[jax.ShapeDtypeStruct((n, N_MAIN_HEADS * LANES), BF16),
                   jax.ShapeDtypeStruct((bsz, D_MAIN, n // bsz), BF16),
                   jax.ShapeDtypeStruct((n, LANES), F32)],
        grid=(bsz, nt),
        in_specs=[pl.BlockSpec((tm, d), row), _const_spec((1, d)), _const_spec(w.shape),
                  _const_spec(w_vt.shape), _const_spec((1, LANES)), _const_spec((1, D_MAIN))],
        out_specs=[pl.BlockSpec((tm, N_MAIN_HEADS * LANES), row),
                   pl.BlockSpec((1, D_MAIN, tm), lambda b, i: (b, 0, i)),
                   pl.BlockSpec((tm, LANES), row)],
        scratch_shapes=[pltpu.VMEM((1, LANES), F32)],
        compiler_params=_params("parallel", "arbitrary"),
        name="shared_kv",
    )(x, kv_norm[None, :], w, w_vt, jnp.pad(b_f, (0, pad))[None, :],
      jnp.tile(k_gain, N_MAIN_HEADS)[None, :])


def _fox_kernel(q_ref, k0_ref, k1_ref, vt_ref, c_ref, gate_ref, qg_ref, o_ref,
                sa_ref, sb_ref, p_ref, acc_ref):
    tq = q_ref.shape[1]
    p = pl.program_id(1)
    qi = pl.program_id(2)
    qn = _head_rms(q_ref[0], qg_ref[...]) * (ATTN_SCALE * LOG2E)
    lane = lax.broadcasted_iota(jnp.int32, qn.shape, 1)
    lo = lane < HEAD_DIM
    c2 = c_ref[0]
    k_refs = (k0_ref, k1_ref)
    q_tiles = []
    for j in range(2):
        own = lo if j == 0 else jnp.logical_not(lo)
        cq = jnp.sum(jnp.where(lane == 2 * p + j, c2, 0.0), axis=-1, keepdims=True)
        bias = _bias_lanes(own, lane, HEAD_DIM * (1 - j), True, _split3(cq))
        q_tiles.append(jnp.where(own, qn, bias).astype(BF16))
    acc_ref[...] = jnp.zeros_like(acc_ref)
    tk = sa_ref.shape[1]
    key_idx = lax.broadcasted_iota(jnp.int32, (tk, LANES), 0)
    qry_idx = lax.broadcasted_iota(jnp.int32, (tk, LANES), 1)

    def scores(block, s_ref):
        start = pl.multiple_of(block * tk, tk)
        for j in range(2):
            s_ref[j] = _dot_nt(k_refs[j][0, pl.ds(start, tk), :], q_tiles[j])

    def step(block, s_ref, slot, carry, key_offset):
        vtb = vt_ref[0, :, pl.ds(pl.multiple_of(block * tk, tk), tk)]
        out = []
        for j in range(2):
            m_old, l_old = carry[j]
            m_parts, l_parts, a_parts = [], [], []
            for g in range(0, tq, LANES):
                gs = slice(g, g + LANES)
                s = s_ref[j, :, gs]
                if key_offset is not None:
                    s = jnp.where(key_idx + key_offset <= qry_idx + g, s, NEG_INF)
                m_new = jnp.maximum(m_old[:, gs], jnp.max(s, axis=0, keepdims=True))
                alpha = jnp.exp2(m_old[:, gs] - m_new)
                e = jnp.exp2(s - m_new)
                l_parts.append(alpha * l_old[:, gs] + jnp.sum(e, axis=0, keepdims=True))
                m_parts.append(m_new)
                a_parts.append(alpha)
                p_ref[slot, j, :, gs] = e.astype(BF16)
            alpha = jnp.concatenate(a_parts, axis=1)
            acc_ref[j] = acc_ref[j] * alpha + jnp.dot(vtb, p_ref[slot, j],
                                                      preferred_element_type=F32)
            out.append((jnp.concatenate(m_parts, axis=1), jnp.concatenate(l_parts, axis=1)))
        return tuple(out)

    def body(t, carry):
        scores(2 * t + 1, sb_ref)
        carry = step(2 * t, sa_ref, 0, carry, None)
        scores(2 * t + 2, sa_ref)
        return step(2 * t + 1, sb_ref, 1, carry, None)

    init = (jnp.full((1, tq), NEG_INF, F32), jnp.zeros((1, tq), F32))
    scores(0, sa_ref)
    carry = lax.fori_loop(0, qi, body, (init, init))
    scores(2 * qi + 1, sb_ref)
    carry = step(2 * qi, sa_ref, 0, carry, 0)
    (_, l0), (_, l1) = step(2 * qi + 1, sb_ref, 1, carry, tk)
    o = jnp.where(lo, (acc_ref[0] / l0).T, (acc_ref[1] / l1).T)
    o_ref[0] = (o * _sigmoid(gate_ref[0])).astype(o_ref.dtype)


def _fox(q, gate, k_tiles, v_t, c2, q_gain, bsz):
    n, _ = q.shape
    t = n // bsz
    tq = ATTN_BLOCK
    blk = pl.BlockSpec((1, tq, LANES), lambda b, p, i: (b, i, p))
    out = pl.pallas_call(
        _fox_kernel,
        out_shape=jax.ShapeDtypeStruct((bsz, t, D_MAIN), BF16),
        grid=(bsz, N_PAIRS, t // tq),
        in_specs=[blk,
                  pl.BlockSpec((1, t, LANES), lambda b, p, i: (b, 0, 2 * p)),
                  pl.BlockSpec((1, t, LANES), lambda b, p, i: (b, 0, 2 * p + 1)),
                  pl.BlockSpec((1, LANES, t), lambda b, p, i: (b, p, 0)),
                  pl.BlockSpec((1, tq, LANES), lambda b, p, i: (b, i, 0)),
                  blk, _const_spec((1, LANES))],
        out_specs=blk,
        scratch_shapes=[pltpu.VMEM((2, tq // 2, tq), F32), pltpu.VMEM((2, tq // 2, tq), F32),
                        pltpu.VMEM((2, 2, tq // 2, tq), BF16), pltpu.VMEM((2, LANES, tq), F32)],
        compiler_params=_params("parallel", "parallel", "arbitrary"),
        name="fox_attn",
    )(q.reshape(bsz, t, D_MAIN), k_tiles.reshape(bsz, t, N_MAIN_HEADS * LANES),
      k_tiles.reshape(bsz, t, N_MAIN_HEADS * LANES), v_t,
      c2.reshape(bsz, t, LANES), gate.reshape(bsz, t, D_MAIN), jnp.tile(q_gain, 2)[None, :])
    return out.reshape(n, D_MAIN)


def _merge_ffn_kernel(x_ref, main_ref, memo_ref, wo_ref, g_ref, wgu_ref, wd_ref, o_ref):
    d_ff = wd_ref.shape[0]
    mixed = (jnp.dot(main_ref[...], wo_ref[:D_MAIN, :], preferred_element_type=F32)
             + jnp.dot(memo_ref[...], wo_ref[D_MAIN:, :], preferred_element_type=F32))
    x = x_ref[...] + mixed
    h = _rms(x, g_ref[...]).astype(BF16)
    acc = x
    for j in range(0, d_ff, FF_CHUNK):
        gj = jnp.dot(h, wgu_ref[:, j:j + FF_CHUNK], preferred_element_type=F32)
        uj = jnp.dot(h, wgu_ref[:, d_ff + j:d_ff + j + FF_CHUNK], preferred_element_type=F32)
        act = (gj * _sigmoid(gj) * uj).astype(BF16)
        acc = acc + jnp.dot(act, wd_ref[j:j + FF_CHUNK, :], preferred_element_type=F32)
    o_ref[...] = acc


def _merge_ffn(x, main, memo, w_out, ffn_norm, w_gate_up, w_down):
    n, d = x.shape
    tm = ROW_BLOCK
    row = lambda i: (i, 0)
    single = pl.Buffered(1)
    return pl.pallas_call(
        _merge_ffn_kernel,
        out_shape=jax.ShapeDtypeStruct((n, d), F32),
        grid=(n // tm,),
        in_specs=[pl.BlockSpec((tm, d), row), pl.BlockSpec((tm, D_MAIN), row),
                  pl.BlockSpec((tm, D_MEMH), row),
                  pl.BlockSpec(w_out.shape, lambda i: (0, 0), pipeline_mode=single),
                  _const_spec((1, d)),
                  pl.BlockSpec(w_gate_up.shape, lambda i: (0, 0), pipeline_mode=single),
                  pl.BlockSpec(w_down.shape, lambda i: (0, 0), pipeline_mode=single)],
        out_specs=pl.BlockSpec((tm, d), row),
        compiler_params=_params("parallel"),
        name="merge_ffn",
    )(x, main, memo, w_out.astype(BF16), ffn_norm[None, :], w_gate_up.astype(BF16),
      w_down.astype(BF16))


def _a_in_weights(w_in, mu):
    d_shift = 3 * D_MAIN + D_DECAY_LORA + D_AAA_LORA + D_GATE_LORA
    pad = D_LORA_G - D_GATE_LORA
    w = jnp.concatenate([w_in[:, :d_shift], jnp.zeros((w_in.shape[0], pad), w_in.dtype),
                         w_in[:, d_shift:]], axis=1)
    return w.astype(BF16), jnp.pad(mu, (0, pad))


def kernel(x, mem, mix_norm, w_out, mem_norm, w_mem_kv, mem_q_gain, mem_k_gain, ffn_norm, w_gate_up, w_down, a_w_in, a_mu, a_w0, a_w_up, a_a0, a_a_up, a_g_up, a_k_k, a_k_a, a_r_k, a_lnx_g, a_lnx_b, kv_norm, w_kv, b_f, k_gain, b_w_in, b_q_gain):
    bsz, t, d = x.shape
    n_a = a_w_in.shape[0]
    n_b = b_w_in.shape[0]
    x = x.reshape(bsz * t, d)
    k_mem, v_mem = _mem_kv(mem, mem_norm, w_mem_kv, mem_k_gain)

    def merge(x, l, main, mem_q):
        memo = _mem_attn(mem_q, k_mem, v_mem, mem_q_gain[l], l, bsz)
        return _merge_ffn(x, main, memo, w_out[l], ffn_norm[l], w_gate_up[l], w_down[l])

    for i in range(n_a):
        w, mu = _a_in_weights(a_w_in[i], a_mu[i])
        u_main, mem_q = _norm_proj(x, mix_norm[i][None, :], w, (D_A_MAIN, D_MEMH), (F32, F32))
        main = _rwkv(u_main, bsz, mu, a_w_up[i], a_a_up[i], a_g_up[i], a_w0[i], a_a0[i],
                     a_k_k[i], a_k_a[i], a_r_k[i], a_lnx_g[i], a_lnx_b[i])
        x = merge(x, i, main, mem_q)

    k_sh, v_sh, c_sh = _shared_kv(x, bsz, kv_norm, w_kv, b_f, k_gain)

    for j in range(n_b):
        l = n_a + j
        q, gate, mem_q = _norm_proj(x, mix_norm[l][None, :], b_w_in[j].astype(BF16),
                                    (D_MAIN, D_MAIN, D_MEMH), (F32, F32, F32))
        main = _fox(q, gate, k_sh, v_sh, c_sh, b_q_gain[j], bsz)
        x = merge(x, l, main, mem_q)

    return x.reshape(bsz, t, d)
```

```python
import functools

import jax
import jax.numpy as jnp
from jax import lax
from jax.experimental import pallas as pl
from jax.experimental.pallas import tpu as pltpu

F32 = jnp.float32
BF16 = jnp.bfloat16
HIGHEST = lax.Precision.HIGHEST

HEAD_DIM = 64
LANES = 128
N_MAIN_HEADS = 12
N_PAIRS = N_MAIN_HEADS // 2
D_MAIN = N_MAIN_HEADS * HEAD_DIM
N_MEM_HEADS = 4
D_MEMH = N_MEM_HEADS * HEAD_DIM
D_DECAY_LORA = 64
D_AAA_LORA = 64
D_GATE_LORA = 160
D_LORA_WA = LANES
D_LORA_G = 2 * LANES
D_A_MAIN = 3 * D_MAIN + D_LORA_WA + D_LORA_G
RMS_EPS = 1e-6
GN_EPS = 64e-5
ATTN_SCALE = HEAD_DIM ** -0.5
NEG_INF = -1e30
LOG2E = 1.4426950408889634
VMEM_LIMIT = 56 * 1024 * 1024

RWKV_CHUNK = 64
RWKV_STEP_CHUNKS = 2
ROW_BLOCK = 512
ATTN_BLOCK = 512
CUMSUM_BLOCK = 128
SUM_ROWS = 16
FF_CHUNK = 1408


def _params(*sem):
    return pltpu.CompilerParams(dimension_semantics=sem, vmem_limit_bytes=VMEM_LIMIT)


def _const_spec(shape):
    nd = len(shape)
    return pl.BlockSpec(shape, lambda *_: (0,) * nd)


def _dot(a, b):
    return jnp.dot(a.astype(BF16), b.astype(BF16), preferred_element_type=F32)


def _dot_nt(a, b):
    return lax.dot_general(a.astype(BF16), b.astype(BF16), (((1,), (1,)), ((), ())),
                           preferred_element_type=F32)


def _dot_tn(a, b):
    return lax.dot_general(a.astype(BF16), b.astype(BF16), (((0,), (0,)), ((), ())),
                           preferred_element_type=F32)


def _dot_f32(a, b):
    return jnp.dot(a, b, preferred_element_type=F32, precision=HIGHEST)


def _sigmoid(x):
    return 1.0 / (1.0 + jnp.exp(-x))


def _softplus(x):
    return jnp.maximum(x, 0.0) + jnp.log(1.0 + jnp.exp(-jnp.abs(x)))


def _rms(x, g):
    return x * lax.rsqrt(jnp.mean(x * x, axis=-1, keepdims=True) + RMS_EPS) * g


def _lo_mask(shape):
    return lax.broadcasted_iota(jnp.int32, shape, len(shape) - 1) < HEAD_DIM


def _pair_sum(x, lo):
    s_lo = jnp.sum(jnp.where(lo, x, 0.0), axis=-1, keepdims=True)
    s_hi = jnp.sum(jnp.where(lo, 0.0, x), axis=-1, keepdims=True)
    return jnp.where(lo, s_lo, s_hi)


def _head_rms(x, gain):
    outs = []
    for j in range(x.shape[-1] // LANES):
        xb = x[:, j * LANES:(j + 1) * LANES]
        lo = _lo_mask(xb.shape)
        ms = _pair_sum(xb * xb, lo) * (1.0 / HEAD_DIM)
        outs.append(xb * lax.rsqrt(ms + RMS_EPS))
    y = outs[0] if len(outs) == 1 else jnp.concatenate(outs, axis=-1)
    return y * gain


def _tril(n, strict):
    r = lax.broadcasted_iota(jnp.int32, (n, n), 0)
    c = lax.broadcasted_iota(jnp.int32, (n, n), 1)
    return (r > c) if strict else (r >= c)


def _norm_proj_kernel(x_ref, g_ref, w_ref, *o_refs, splits):
    h = _rms(x_ref[...], g_ref[...]).astype(BF16)
    c0 = 0
    for o_ref, width in zip(o_refs, splits):
        for j in range(0, width, 2 * LANES):
            wj = min(2 * LANES, width - j)
            o_ref[:, j:j + wj] = jnp.dot(h, w_ref[:, c0 + j:c0 + j + wj],
                                         preferred_element_type=F32).astype(o_ref.dtype)
        c0 += width


def _norm_proj(x, g, w, splits, dtypes):
    n, d = x.shape
    tm = ROW_BLOCK
    return pl.pallas_call(
        functools.partial(_norm_proj_kernel, splits=splits),
        out_shape=[jax.ShapeDtypeStruct((n, s), dt) for s, dt in zip(splits, dtypes)],
        grid=(n // tm,),
        in_specs=[pl.BlockSpec((tm, d), lambda i: (i, 0)),
                  _const_spec((1, d)),
                  _const_spec(w.shape)],
        out_specs=[pl.BlockSpec((tm, s), lambda i: (i, 0)) for s in splits],
        compiler_params=_params("parallel"),
        name="norm_proj",
    )(x, g, w)


def _mem_kv_kernel(mem_ref, g_ref, w_ref, kg_ref, k_ref, v_ref):
    h = _rms(mem_ref[0], g_ref[0]).astype(BF16)
    kv = jnp.dot(h, w_ref[0], preferred_element_type=F32)
    k_ref[0, 0] = _head_rms(kv[:, :D_MEMH], kg_ref[0]).astype(BF16)
    v_ref[0, 0] = kv[:, D_MEMH:].astype(BF16)


def _mem_kv(mem, mem_norm, w_mem_kv, mem_k_gain):
    b, m, d = mem.shape
    depth = w_mem_kv.shape[0]
    out = jax.ShapeDtypeStruct((depth, b, m, D_MEMH), BF16)
    return pl.pallas_call(
        _mem_kv_kernel,
        out_shape=[out, out],
        grid=(depth, b),
        in_specs=[pl.BlockSpec((1, m, d), lambda l, i: (i, 0, 0)),
                  pl.BlockSpec((1, 1, d), lambda l, i: (l, 0, 0)),
                  pl.BlockSpec((1, d, 2 * D_MEMH), lambda l, i: (l, 0, 0)),
                  pl.BlockSpec((1, 1, D_MEMH), lambda l, i: (l, 0, 0))],
        out_specs=[pl.BlockSpec((1, 1, m, D_MEMH), lambda l, i: (l, i, 0, 0))] * 2,
        compiler_params=_params("parallel", "parallel"),
        name="mem_kv",
    )(mem, mem_norm[:, None, :], w_mem_kv.astype(BF16),
      jnp.tile(mem_k_gain, (1, N_MEM_HEADS))[:, None, :])


def _mem_attn_kernel(q_ref, k_ref, v_ref, qg_ref, o_ref):
    qn = _head_rms(q_ref[0], qg_ref[...]) * ATTN_SCALE
    k = k_ref[0, 0]
    v = v_ref[0, 0]
    lane = lax.broadcasted_iota(jnp.int32, qn.shape, 1)
    out = jnp.zeros(qn.shape, F32)
    for h in range(N_MEM_HEADS):
        sel = jnp.logical_and(lane >= h * HEAD_DIM, lane < (h + 1) * HEAD_DIM)
        s = _dot_nt(jnp.where(sel, qn, 0.0), k)
        e = jnp.exp(s - jnp.max(s, axis=-1, keepdims=True))
        o = _dot(e, v) / jnp.sum(e, axis=-1, keepdims=True)
        out = jnp.where(sel, o, out)
    o_ref[0] = out.astype(o_ref.dtype)


def _mem_attn(q_cols, k_mem, v_mem, q_gain, layer, bsz):
    n, _ = q_cols.shape
    t = n // bsz
    tq = ATTN_BLOCK
    m = k_mem.shape[2]
    kv_spec = pl.BlockSpec((1, 1, m, D_MEMH), lambda b, i: (layer, b, 0, 0))
    out = pl.pallas_call(
        _mem_attn_kernel,
        out_shape=jax.ShapeDtypeStruct((bsz, t, D_MEMH), BF16),
        grid=(bsz, t // tq),
        in_specs=[pl.BlockSpec((1, tq, D_MEMH), lambda b, i: (b, i, 0)),
                  kv_spec, kv_spec, _const_spec((1, D_MEMH))],
        out_specs=pl.BlockSpec((1, tq, D_MEMH), lambda b, i: (b, i, 0)),
        compiler_params=_params("parallel", "parallel"),
        name="mem_attn",
    )(q_cols.reshape(bsz, t, D_MEMH), k_mem, v_mem, jnp.tile(q_gain, N_MEM_HEADS)[None, :])
    return out.reshape(n, D_MEMH)


def _rwkv_kernel(u_ref, mu_ref, wup_ref, aup_ref, gup_ref, w0_ref, a0_ref, kk_ref, ka_ref,
                 rk_ref, lng_ref, lnb_ref, o_ref, state_ref, prev_ref):
    L = RWKV_CHUNK
    L2 = 2 * L
    assert L2 == LANES
    rows = u_ref.shape[0]
    chunks = range(rows // L)
    c = pl.program_id(1)

    @pl.when(c == 0)
    def _():
        state_ref[...] = jnp.zeros_like(state_ref)
        prev_ref[...] = jnp.zeros_like(prev_ref)

    u = u_ref[...]
    row = lax.broadcasted_iota(jnp.int32, u.shape, 0)
    shifted = jnp.where(row == 0, prev_ref[...], pltpu.roll(u, 1, axis=0))
    prev_ref[...] = u[rows - 1:rows, :]
    us = u + (shifted - u) * mu_ref[...]

    r = us[:, 0:D_MAIN]
    k = us[:, D_MAIN:2 * D_MAIN]
    v = us[:, 2 * D_MAIN:3 * D_MAIN]
    x_wa = us[:, 3 * D_MAIN:3 * D_MAIN + D_LORA_WA]
    x_g = us[:, 3 * D_MAIN + D_LORA_WA:]

    w_log = -_softplus(-(w0_ref[...] + _dot(jnp.tanh(x_wa), wup_ref[...]))) - 0.5
    log_w = -jnp.exp(w_log)
    lr = _sigmoid(a0_ref[...] + _dot(x_wa, aup_ref[...]))
    gate = _dot(_sigmoid(x_g), gup_ref[...])
    kk_raw = k * kk_ref[...]
    k = k * (1.0 + (lr - 1.0) * ka_ref[...])

    tok_i = lax.broadcasted_iota(jnp.int32, (rows, rows), 0)
    tok_j = lax.broadcasted_iota(jnp.int32, (rows, rows), 1)
    shift = L.bit_length() - 1
    same_chunk_tril = jnp.logical_and(tok_i >= tok_j, (tok_i >> shift) == (tok_j >> shift))
    cum = _dot_f32(same_chunk_tril.astype(F32), log_w)
    cum_end = jnp.concatenate(
        [jnp.broadcast_to(cum[(ci + 1) * L - 1:(ci + 1) * L, :], (L, D_MAIN)) for ci in chunks], axis=0)
    g_incl = jnp.exp(cum)
    g_excl = jnp.exp(cum - log_w)
    g_inv = jnp.exp(-cum)
    g_tail = jnp.exp(cum_end - cum)

    lo = _lo_mask((rows, LANES))
    own = (lax.broadcasted_iota(jnp.int32, (L2, LANES), 0) < L) == _lo_mask((L2, LANES))
    tok_r = lax.broadcasted_iota(jnp.int32, (L2, L2), 0) & (L - 1)
    tok_c = lax.broadcasted_iota(jnp.int32, (L2, L2), 1) & (L - 1)
    strict = tok_r > tok_c
    incl = tok_r >= tok_c
    eye = (lax.broadcasted_iota(jnp.int32, (L2, L2), 0)
           == lax.broadcasted_iota(jnp.int32, (L2, L2), 1)).astype(F32)

    def stack(x):
        return jnp.where(own, jnp.concatenate([x, x], axis=0), 0.0)

    pairs = range(N_PAIRS)
    sls = [slice(p * LANES, (p + 1) * LANES) for p in pairs]
    units = [(ci, p) for ci in chunks for p in pairs]
    kkn, b = [], []
    for sl in sls:
        kk = kk_raw[:, sl]
        kkn.append(kk * lax.rsqrt(jnp.maximum(_pair_sum(kk * kk, lo), 1e-24)))
        b.append(kkn[-1] * lr[:, sl])
    a_dec = [-kkn[p] * g_excl[:, sls[p]] for p in pairs]
    r_dec = [r[:, sls[p]] * g_incl[:, sls[p]] for p in pairs]
    b_inv = [b[p] * g_inv[:, sls[p]] for p in pairs]
    k_inv = [k[:, sls[p]] * g_inv[:, sls[p]] for p in pairs]
    b_tail = [b[p] * g_tail[:, sls[p]] for p in pairs]
    k_tail = [k[:, sls[p]] * g_tail[:, sls[p]] for p in pairs]

    def chunk_rows(ci):
        return slice(ci * L, (ci + 1) * L)

    ar, v2, power, a_ak, a_r = {}, {}, {}, {}, {}
    for ci, p in units:
        rs = chunk_rows(ci)
        ar[ci, p] = jnp.concatenate([stack(a_dec[p][rs]), stack(r_dec[p][rs])],
                                    axis=0).astype(BF16)
        v2[ci, p] = stack(v[rs, sls[p]]).astype(BF16)
        bk_inv = jnp.concatenate([stack(b_inv[p][rs]), stack(k_inv[p][rs])], axis=0)
        g = _dot_nt(ar[ci, p], bk_inv)
        power[ci, p] = jnp.where(strict, g[:L2, :L2], 0.0)
        a_ak[ci, p] = jnp.where(strict, g[:L2, L2:], 0.0).astype(BF16)
        a_r[ci, p] = jnp.concatenate([jnp.where(incl, g[L2:, :L2], 0.0),
                                      jnp.where(incl, g[L2:, L2:], 0.0)],
                                     axis=1).astype(BF16)

    inv = {un: eye + power[un] for un in units}
    for _ in range(L.bit_length() - 2):
        power = {un: _dot(power[un], power[un]) for un in units}
        inv = {un: inv[un] + _dot(power[un], inv[un]) for un in units}
    from_v = {un: _dot(a_ak[un], v2[un]) for un in units}

    states = [state_ref[p] for p in pairs]
    y2 = {}
    for ci in chunks:
        rs = chunk_rows(ci)
        from_state = [_dot_nt(ar[ci, p], states[p]) for p in pairs]
        uv = [jnp.concatenate([_dot(inv[ci, p], from_state[p][:L2] + from_v[ci, p]).astype(BF16),
                               v2[ci, p]], axis=0) for p in pairs]
        for p in pairs:
            y2[ci, p] = from_state[p][L2:] + _dot(a_r[ci, p], uv[p])
        new_states = []
        for p in pairs:
            bk = jnp.concatenate([stack(b_tail[p][rs]), stack(k_tail[p][rs])], axis=0)
            decay = g_incl[(ci + 1) * L - 1:(ci + 1) * L, sls[p]]
            new_states.append(states[p] * decay + _dot_tn(uv[p], bk))
        states = new_states
    for p in pairs:
        state_ref[p] = states[p]

    for p in pairs:
        sl = sls[p]
        y = jnp.concatenate([y2[ci, p][:L] + y2[ci, p][L:] for ci in chunks], axis=0)
        mean = _pair_sum(y, lo) * (1.0 / HEAD_DIM)
        yc = y - mean
        var = _pair_sum(yc * yc, lo) * (1.0 / HEAD_DIM)
        yn = yc * lax.rsqrt(var + GN_EPS) * lng_ref[:, sl] + lnb_ref[:, sl]
        bonus = _pair_sum(r[:, sl] * k[:, sl] * rk_ref[:, sl], lo) * v[:, sl]
        o_ref[:, sl] = ((yn + bonus) * gate[:, sl]).astype(o_ref.dtype)


def _rwkv(u_main, bsz, mu, w_up, a_up, g_up, w0, a0, k_k, k_a, r_k, lnx_g, lnx_b):
    n, width = u_main.shape
    step_rows = RWKV_CHUNK * RWKV_STEP_CHUNKS
    nc = n // bsz // step_rows
    rows = [w0, a0, k_k, k_a, r_k.reshape(-1), lnx_g, lnx_b]
    zeros_wa = jnp.zeros((D_DECAY_LORA, D_MAIN), F32)
    wup = jnp.concatenate([w_up, zeros_wa], axis=0)
    aup = jnp.concatenate([zeros_wa, a_up], axis=0)
    gup = jnp.concatenate([g_up, jnp.zeros((D_LORA_G - D_GATE_LORA, D_MAIN), F32)], axis=0)
    return pl.pallas_call(
        _rwkv_kernel,
        out_shape=jax.ShapeDtypeStruct((n, D_MAIN), BF16),
        grid=(bsz, nc),
        in_specs=[pl.BlockSpec((step_rows, width), lambda b, c: (b * nc + c, 0)),
                  _const_spec((1, width)),
                  _const_spec(wup.shape), _const_spec(aup.shape), _const_spec(gup.shape)]
                 + [_const_spec((1, D_MAIN))] * len(rows),
        out_specs=pl.BlockSpec((step_rows, D_MAIN), lambda b, c: (b * nc + c, 0)),
        scratch_shapes=[pltpu.VMEM((N_PAIRS, LANES, LANES), F32),
                        pltpu.VMEM((1, width), F32)],
        compiler_params=_params("parallel", "arbitrary"),
        name="rwkv7",
    )(u_main, mu[None, :], wup, aup, gup, *[x[None, :] for x in rows])


def _split3(x):
    hi = x.astype(BF16).astype(F32)
    mid = (x - hi).astype(BF16).astype(F32)
    lo = (x - hi - mid).astype(BF16).astype(F32)
    return hi, mid, lo


def _bias_lanes(own, lane, base, ones_first, terms):
    one_at = base if ones_first else base + 3
    term_at = base + 3 if ones_first else base
    out = jnp.where(jnp.logical_and(lane >= one_at, lane < one_at + 3), 1.0, 0.0)
    for i, term in enumerate(terms):
        out = jnp.where(lane == term_at + i, term, out)
    return jnp.where(own, 0.0, out)


def _shared_kv_kernel(x_ref, g_ref, w_ref, wvt_ref, bf_ref, kg_ref, k_ref, vt_ref, c_ref,
                      carry_ref):
    @pl.when(pl.program_id(1) == 0)
    def _():
        carry_ref[...] = jnp.zeros_like(carry_ref)

    tm = x_ref.shape[0]
    h = _rms(x_ref[...], g_ref[...]).astype(BF16)
    vt_ref[0] = _dot_nt(wvt_ref[...], h).astype(BF16)
    logits = jnp.dot(h, w_ref[:, D_MAIN:], preferred_element_type=F32) + bf_ref[...]
    log_f = -_softplus(-logits)
    tri = _tril(CUMSUM_BLOCK, strict=False).astype(F32)
    carry = carry_ref[...]
    for j in range(0, tm, CUMSUM_BLOCK):
        cj = _dot_f32(tri, log_f[j:j + CUMSUM_BLOCK]) + carry
        c_ref[j:j + CUMSUM_BLOCK, :] = cj * LOG2E
        carry = cj[CUMSUM_BLOCK - 1:CUMSUM_BLOCK, :]
    carry_ref[...] = carry

    c2 = c_ref[...]
    lane = lax.broadcasted_iota(jnp.int32, (tm, LANES), 1)
    lo = lane < HEAD_DIM
    for p in range(N_PAIRS):
        sl = slice(p * LANES, (p + 1) * LANES)
        kn = _head_rms(jnp.dot(h, w_ref[:, sl], preferred_element_type=F32), kg_ref[:, sl])
        for j in range(2):
            head = 2 * p + j
            own = lo if j == 0 else jnp.logical_not(lo)
            ck = jnp.sum(jnp.where(lane == head, c2, 0.0), axis=-1, keepdims=True)
            bias = _bias_lanes(own, lane, HEAD_DIM * (1 - j), False, [-t for t in _split3(ck)])
            k_ref[:, head * LANES:(head + 1) * LANES] = jnp.where(own, kn, bias).astype(BF16)


def _shared_kv(x, bsz, kv_norm, w_kv, b_f, k_gain):
    n, d = x.shape
    tm = ROW_BLOCK
    nt = n // bsz // tm
    pad = LANES - N_MAIN_HEADS
    w = jnp.concatenate([w_kv[:, :D_MAIN], jnp.pad(w_kv[:, 2 * D_MAIN:], ((0, 0), (0, pad)))],
                        axis=1).astype(BF16)
    w_vt = w_kv[:, D_MAIN:2 * D_MAIN].T.astype(BF16)
    row = lambda b, i: (b * nt + i, 0)
    return pl.pallas_call(
        _shared_kv_kernel,
        out_shape=[jax.ShapeDtypeStruct((n, N_MAIN_HEADS * LANES), BF16),
                   jax.ShapeDtypeStruct((bsz, D_MAIN, n // bsz), BF16),
                   jax.ShapeDtypeStruct((n, LANES), F32)],
        grid=(bsz, nt),
        in_specs=[pl.BlockSpec((tm, d), row), _const_spec((1, d)), _const_spec(w.shape),
                  _const_spec(w_vt.shape), _const_spec((1, LANES)), _const_spec((1, D_MAIN))],
        out_specs=[pl.BlockSpec((tm, N_MAIN_HEADS * LANES), row),
                   pl.BlockSpec((1, D_MAIN, tm), lambda b, i: (b, 0, i)),
                   pl.BlockSpec((tm, LANES), row)],
        scratch_shapes=[pltpu.VMEM((1, LANES), F32)],
        compiler_params=_params("parallel", "arbitrary"),
        name="shared_kv",
    )(x, kv_norm[None, :], w, w_vt, jnp.pad(b_f, (0, pad))[None, :],
      jnp.tile(k_gain, N_MAIN_HEADS)[None, :])


def _fox_kernel(q_ref, k0_ref, k1_ref, vt_ref, c_ref, gate_ref, qg_ref, o_ref,
                sa_ref, sb_ref, p_ref, acc_ref):
    tq = q_ref.shape[1]
    p = pl.program_id(1)
    qi = pl.program_id(2)
    qn = _head_rms(q_ref[0], qg_ref[...]) * (ATTN_SCALE * LOG2E)
    lane = lax.broadcasted_iota(jnp.int32, qn.shape, 1)
    lo = lane < HEAD_DIM
    c2 = c_ref[0]
    k_refs = (k0_ref, k1_ref)
    q_tiles = []
    for j in range(2):
        own = lo if j == 0 else jnp.logical_not(lo)
        cq = jnp.sum(jnp.where(lane == 2 * p + j, c2, 0.0), axis=-1, keepdims=True)
        bias = _bias_lanes(own, lane, HEAD_DIM * (1 - j), True, _split3(cq))
        q_tiles.append(jnp.where(own, qn, bias).astype(BF16))
    acc_ref[...] = jnp.zeros_like(acc_ref)
    tk = sa_ref.shape[1]
    key_idx = lax.broadcasted_iota(jnp.int32, (tk, LANES), 0)
    qry_idx = lax.broadcasted_iota(jnp.int32, (tk, LANES), 1)

    def scores(block, s_ref):
        start = pl.multiple_of(block * tk, tk)
        for j in range(2):
            s_ref[j] = _dot_nt(k_refs[j][0, pl.ds(start, tk), :], q_tiles[j])

    def step(block, s_ref, slot, carry, key_offset):
        vtb = vt_ref[0, :, pl.ds(pl.multiple_of(block * tk, tk), tk)]
        vtb = jnp.concatenate([vtb, jnp.ones((SUM_ROWS, tk), BF16)], axis=0)
        out = []
        for j in range(2):
            m_old = carry[j]
            m_parts, a_parts = [], []
            for g in range(0, tq, LANES):
                gs = slice(g, g + LANES)
                s = s_ref[j, :, gs]
                if key_offset is not None:
                    s = jnp.where(key_idx + key_offset <= qry_idx + g, s, NEG_INF)
                m_new = jnp.maximum(m_old[:, gs], jnp.max(s, axis=0, keepdims=True))
                m_parts.append(m_new)
                a_parts.append(jnp.exp2(m_old[:, gs] - m_new))
                p_ref[slot, j, :, gs] = jnp.exp2((s - m_new).astype(BF16))
            alpha = jnp.concatenate(a_parts, axis=1)
            acc_ref[j] = acc_ref[j] * alpha + jnp.dot(vtb, p_ref[slot, j],
                                                      preferred_element_type=F32)
            out.append(jnp.concatenate(m_parts, axis=1))
        return tuple(out)

    def body(t, carry):
        scores(2 * t + 1, sb_ref)
        carry = step(2 * t, sa_ref, 0, carry, None)
        scores(2 * t + 2, sa_ref)
        return step(2 * t + 1, sb_ref, 1, carry, None)

    init = jnp.full((1, tq), NEG_INF, F32)
    scores(0, sa_ref)
    carry = lax.fori_loop(0, qi, body, (init, init))
    scores(2 * qi + 1, sb_ref)
    carry = step(2 * qi, sa_ref, 0, carry, 0)
    step(2 * qi + 1, sb_ref, 1, carry, tk)
    heads = [(acc_ref[j, :LANES, :] / acc_ref[j, LANES:LANES + 1, :]).T for j in range(2)]
    o = jnp.where(lo, heads[0], heads[1])
    o_ref[0] = (o * _sigmoid(gate_ref[0])).astype(o_ref.dtype)


def _fox(q, gate, k_tiles, v_t, c2, q_gain, bsz):
    n, _ = q.shape
    t = n // bsz
    tq = ATTN_BLOCK
    blk = pl.BlockSpec((1, tq, LANES), lambda b, p, i: (b, i, p))
    out = pl.pallas_call(
        _fox_kernel,
        out_shape=jax.ShapeDtypeStruct((bsz, t, D_MAIN), BF16),
        grid=(bsz, N_PAIRS, t // tq),
        in_specs=[blk,
                  pl.BlockSpec((1, t, LANES), lambda b, p, i: (b, 0, 2 * p)),
                  pl.BlockSpec((1, t, LANES), lambda b, p, i: (b, 0, 2 * p + 1)),
                  pl.BlockSpec((1, LANES, t), lambda b, p, i: (b, p, 0)),
                  pl.BlockSpec((1, tq, LANES), lambda b, p, i: (b, i, 0)),
                  blk, _const_spec((1, LANES))],
        out_specs=blk,
        scratch_shapes=[pltpu.VMEM((2, tq // 2, tq), F32), pltpu.VMEM((2, tq // 2, tq), F32),
                        pltpu.VMEM((2, 2, tq // 2, tq), BF16),
                        pltpu.VMEM((2, LANES + SUM_ROWS, tq), F32)],
        compiler_params=_params("parallel", "parallel", "arbitrary"),
        name="fox_attn",
    )(q.reshape(bsz, t, D_MAIN), k_tiles.reshape(bsz, t, N_MAIN_HEADS * LANES),
      k_tiles.reshape(bsz, t, N_MAIN_HEADS * LANES), v_t,
      c2.reshape(bsz, t, LANES), gate.reshape(bsz, t, D_MAIN), jnp.tile(q_gain, 2)[None, :])
    return out.reshape(n, D_MAIN)


def _merge_ffn_kernel(x_ref, main_ref, memo_ref, wo_ref, g_ref, wgu_ref, wd_ref, o_ref):
    d_ff = wd_ref.shape[0]
    mixed = (jnp.dot(main_ref[...], wo_ref[:D_MAIN, :], preferred_element_type=F32)
             + jnp.dot(memo_ref[...], wo_ref[D_MAIN:, :], preferred_element_type=F32))
    x = x_ref[...] + mixed
    h = _rms(x, g_ref[...]).astype(BF16)
    acc = x
    for j in range(0, d_ff, FF_CHUNK):
        gj = jnp.dot(h, wgu_ref[:, j:j + FF_CHUNK], preferred_element_type=F32)
        uj = jnp.dot(h, wgu_ref[:, d_ff + j:d_ff + j + FF_CHUNK], preferred_element_type=F32)
        act = (gj * _sigmoid(gj) * uj).astype(BF16)
        acc = acc + jnp.dot(act, wd_ref[j:j + FF_CHUNK, :], preferred_element_type=F32)
    o_ref[...] = acc


def _merge_ffn(x, main, memo, w_out, ffn_norm, w_gate_up, w_down):
    n, d = x.shape
    tm = ROW_BLOCK
    row = lambda i: (i, 0)
    single = pl.Buffered(1)
    return pl.pallas_call(
        _merge_ffn_kernel,
        out_shape=jax.ShapeDtypeStruct((n, d), F32),
        grid=(n // tm,),
        in_specs=[pl.BlockSpec((tm, d), row), pl.BlockSpec((tm, D_MAIN), row),
                  pl.BlockSpec((tm, D_MEMH), row),
                  pl.BlockSpec(w_out.shape, lambda i: (0, 0), pipeline_mode=single),
                  _const_spec((1, d)),
                  pl.BlockSpec(w_gate_up.shape, lambda i: (0, 0), pipeline_mode=single),
                  pl.BlockSpec(w_down.shape, lambda i: (0, 0), pipeline_mode=single)],
        out_specs=pl.BlockSpec((tm, d), row),
        compiler_params=_params("parallel"),
        name="merge_ffn",
    )(x, main, memo, w_out.astype(BF16), ffn_norm[None, :], w_gate_up.astype(BF16),
      w_down.astype(BF16))


def _a_in_weights(w_in, mu):
    d_shift = 3 * D_MAIN + D_DECAY_LORA + D_AAA_LORA + D_GATE_LORA
    pad = D_LORA_G - D_GATE_LORA
    w = jnp.concatenate([w_in[:, :d_shift], jnp.zeros((w_in.shape[0], pad), w_in.dtype),
                         w_in[:, d_shift:]], axis=1)
    return w.astype(BF16), jnp.pad(mu, (0, pad))


def kernel(x, mem, mix_norm, w_out, mem_norm, w_mem_kv, mem_q_gain, mem_k_gain, ffn_norm, w_gate_up, w_down, a_w_in, a_mu, a_w0, a_w_up, a_a0, a_a_up, a_g_up, a_k_k, a_k_a, a_r_k, a_lnx_g, a_lnx_b, kv_norm, w_kv, b_f, k_gain, b_w_in, b_q_gain):
    bsz, t, d = x.shape
    n_a = a_w_in.shape[0]
    n_b = b_w_in.shape[0]
    x = x.reshape(bsz * t, d)
    k_mem, v_mem = _mem_kv(mem, mem_norm, w_mem_kv, mem_k_gain)

    def merge(x, l, main, mem_q):
        memo = _mem_attn(mem_q, k_mem, v_mem, mem_q_gain[l], l, bsz)
        return _merge_ffn(x, main, memo, w_out[l], ffn_norm[l], w_gate_up[l], w_down[l])

    for i in range(n_a):
        w, mu = _a_in_weights(a_w_in[i], a_mu[i])
        u_main, mem_q = _norm_proj(x, mix_norm[i][None, :], w, (D_A_MAIN, D_MEMH), (F32, F32))
        main = _rwkv(u_main, bsz, mu, a_w_up[i], a_a_up[i], a_g_up[i], a_w0[i], a_a0[i],
                     a_k_k[i], a_k_a[i], a_r_k[i], a_lnx_g[i], a_lnx_b[i])
        x = merge(x, i, main, mem_q)

    k_sh, v_sh, c_sh = _shared_kv(x, bsz, kv_norm, w_kv, b_f, k_gain)

    for j in range(n_b):
        l = n_a + j
        q, gate, mem_q = _norm_proj(x, mix_norm[l][None, :], b_w_in[j].astype(BF16),
                                    (D_MAIN, D_MAIN, D_MEMH), (F32, F32, F32))
        main = _fox(q, gate, k_sh, v_sh, c_sh, b_q_gain[j], bsz)
        x = merge(x, l, main, mem_q)

    return x.reshape(bsz, t, d)
```

```python
import functools

import jax
import jax.numpy as jnp
from jax import lax
from jax.experimental import pallas as pl
from jax.experimental.pallas import tpu as pltpu

F32 = jnp.float32
BF16 = jnp.bfloat16
HIGHEST = lax.Precision.HIGHEST

HEAD_DIM = 64
LANES = 128
N_MAIN_HEADS = 12
N_PAIRS = N_MAIN_HEADS // 2
D_MAIN = N_MAIN_HEADS * HEAD_DIM
N_MEM_HEADS = 4
D_MEMH = N_MEM_HEADS * HEAD_DIM
D_DECAY_LORA = 64
D_AAA_LORA = 64
D_GATE_LORA = 160
D_LORA_WA = LANES
D_LORA_G = 2 * LANES
D_A_MAIN = 3 * D_MAIN + D_LORA_WA + D_LORA_G
RMS_EPS = 1e-6
GN_EPS = 64e-5
ATTN_SCALE = HEAD_DIM ** -0.5
NEG_INF = -1e30
LOG2E = 1.4426950408889634
VMEM_LIMIT = 56 * 1024 * 1024

RWKV_CHUNK = 64
RWKV_STEP_CHUNKS = 2
ROW_BLOCK = 512
ATTN_BLOCK = 512
CUMSUM_BLOCK = 128
SUM_ROWS = 16
UNDERFLOW_LOG2 = 160.0
FF_CHUNK = 1408


def _params(*sem):
    return pltpu.CompilerParams(dimension_semantics=sem, vmem_limit_bytes=VMEM_LIMIT)


def _const_spec(shape):
    nd = len(shape)
    return pl.BlockSpec(shape, lambda *_: (0,) * nd)


def _dot(a, b):
    return jnp.dot(a.astype(BF16), b.astype(BF16), preferred_element_type=F32)


def _dot_nt(a, b):
    return lax.dot_general(a.astype(BF16), b.astype(BF16), (((1,), (1,)), ((), ())),
                           preferred_element_type=F32)


def _dot_tn(a, b):
    return lax.dot_general(a.astype(BF16), b.astype(BF16), (((0,), (0,)), ((), ())),
                           preferred_element_type=F32)


def _dot_f32(a, b):
    return jnp.dot(a, b, preferred_element_type=F32, precision=HIGHEST)


def _sigmoid(x):
    return 1.0 / (1.0 + jnp.exp(-x))


def _softplus(x):
    return jnp.maximum(x, 0.0) + jnp.log(1.0 + jnp.exp(-jnp.abs(x)))


def _rms(x, g):
    return x * lax.rsqrt(jnp.mean(x * x, axis=-1, keepdims=True) + RMS_EPS) * g


def _lo_mask(shape):
    return lax.broadcasted_iota(jnp.int32, shape, len(shape) - 1) < HEAD_DIM


def _pair_sum(x, lo):
    s_lo = jnp.sum(jnp.where(lo, x, 0.0), axis=-1, keepdims=True)
    s_hi = jnp.sum(jnp.where(lo, 0.0, x), axis=-1, keepdims=True)
    return jnp.where(lo, s_lo, s_hi)


def _head_rms(x, gain):
    outs = []
    for j in range(x.shape[-1] // LANES):
        xb = x[:, j * LANES:(j + 1) * LANES]
        lo = _lo_mask(xb.shape)
        ms = _pair_sum(xb * xb, lo) * (1.0 / HEAD_DIM)
        outs.append(xb * lax.rsqrt(ms + RMS_EPS))
    y = outs[0] if len(outs) == 1 else jnp.concatenate(outs, axis=-1)
    return y * gain


def _tril(n, strict):
    r = lax.broadcasted_iota(jnp.int32, (n, n), 0)
    c = lax.broadcasted_iota(jnp.int32, (n, n), 1)
    return (r > c) if strict else (r >= c)


def _norm_proj_kernel(x_ref, g_ref, w_ref, *o_refs, splits):
    h = _rms(x_ref[...], g_ref[...]).astype(BF16)
    c0 = 0
    for o_ref, width in zip(o_refs, splits):
        for j in range(0, width, 2 * LANES):
            wj = min(2 * LANES, width - j)
            o_ref[:, j:j + wj] = jnp.dot(h, w_ref[:, c0 + j:c0 + j + wj],
                                         preferred_element_type=F32).astype(o_ref.dtype)
        c0 += width


def _norm_proj(x, g, w, splits, dtypes):
    n, d = x.shape
    tm = ROW_BLOCK
    return pl.pallas_call(
        functools.partial(_norm_proj_kernel, splits=splits),
        out_shape=[jax.ShapeDtypeStruct((n, s), dt) for s, dt in zip(splits, dtypes)],
        grid=(n // tm,),
        in_specs=[pl.BlockSpec((tm, d), lambda i: (i, 0)),
                  _const_spec((1, d)),
                  _const_spec(w.shape)],
        out_specs=[pl.BlockSpec((tm, s), lambda i: (i, 0)) for s in splits],
        compiler_params=_params("parallel"),
        name="norm_proj",
    )(x, g, w)


def _mem_kv_kernel(mem_ref, g_ref, w_ref, kg_ref, k_ref, v_ref):
    h = _rms(mem_ref[0], g_ref[0]).astype(BF16)
    kv = jnp.dot(h, w_ref[0], preferred_element_type=F32)
    k_ref[0, 0] = _head_rms(kv[:, :D_MEMH], kg_ref[0]).astype(BF16)
    v_ref[0, 0] = kv[:, D_MEMH:].astype(BF16)


def _mem_kv(mem, mem_norm, w_mem_kv, mem_k_gain):
    b, m, d = mem.shape
    depth = w_mem_kv.shape[0]
    out = jax.ShapeDtypeStruct((depth, b, m, D_MEMH), BF16)
    return pl.pallas_call(
        _mem_kv_kernel,
        out_shape=[out, out],
        grid=(depth, b),
        in_specs=[pl.BlockSpec((1, m, d), lambda l, i: (i, 0, 0)),
                  pl.BlockSpec((1, 1, d), lambda l, i: (l, 0, 0)),
                  pl.BlockSpec((1, d, 2 * D_MEMH), lambda l, i: (l, 0, 0)),
                  pl.BlockSpec((1, 1, D_MEMH), lambda l, i: (l, 0, 0))],
        out_specs=[pl.BlockSpec((1, 1, m, D_MEMH), lambda l, i: (l, i, 0, 0))] * 2,
        compiler_params=_params("parallel", "parallel"),
        name="mem_kv",
    )(mem, mem_norm[:, None, :], w_mem_kv.astype(BF16),
      jnp.tile(mem_k_gain, (1, N_MEM_HEADS))[:, None, :])


def _mem_attn_kernel(q_ref, k_ref, v_ref, qg_ref, o_ref):
    qn = _head_rms(q_ref[0], qg_ref[...]) * ATTN_SCALE
    k = k_ref[0, 0]
    v = v_ref[0, 0]
    lane = lax.broadcasted_iota(jnp.int32, qn.shape, 1)
    out = jnp.zeros(qn.shape, F32)
    for h in range(N_MEM_HEADS):
        sel = jnp.logical_and(lane >= h * HEAD_DIM, lane < (h + 1) * HEAD_DIM)
        s = _dot_nt(jnp.where(sel, qn, 0.0), k)
        e = jnp.exp(s - jnp.max(s, axis=-1, keepdims=True))
        o = _dot(e, v) / jnp.sum(e, axis=-1, keepdims=True)
        out = jnp.where(sel, o, out)
    o_ref[0] = out.astype(o_ref.dtype)


def _mem_attn(q_cols, k_mem, v_mem, q_gain, layer, bsz):
    n, _ = q_cols.shape
    t = n // bsz
    tq = ATTN_BLOCK
    m = k_mem.shape[2]
    kv_spec = pl.BlockSpec((1, 1, m, D_MEMH), lambda b, i: (layer, b, 0, 0))
    out = pl.pallas_call(
        _mem_attn_kernel,
        out_shape=jax.ShapeDtypeStruct((bsz, t, D_MEMH), BF16),
        grid=(bsz, t // tq),
        in_specs=[pl.BlockSpec((1, tq, D_MEMH), lambda b, i: (b, i, 0)),
                  kv_spec, kv_spec, _const_spec((1, D_MEMH))],
        out_specs=pl.BlockSpec((1, tq, D_MEMH), lambda b, i: (b, i, 0)),
        compiler_params=_params("parallel", "parallel"),
        name="mem_attn",
    )(q_cols.reshape(bsz, t, D_MEMH), k_mem, v_mem, jnp.tile(q_gain, N_MEM_HEADS)[None, :])
    return out.reshape(n, D_MEMH)


def _rwkv_kernel(u_ref, mu_ref, wup_ref, aup_ref, gup_ref, w0_ref, a0_ref, kk_ref, ka_ref,
                 rk_ref, lng_ref, lnb_ref, o_ref, state_ref, prev_ref):
    L = RWKV_CHUNK
    L2 = 2 * L
    assert L2 == LANES
    rows = u_ref.shape[0]
    chunks = range(rows // L)
    c = pl.program_id(1)

    @pl.when(c == 0)
    def _():
        state_ref[...] = jnp.zeros_like(state_ref)
        prev_ref[...] = jnp.zeros_like(prev_ref)

    u = u_ref[...]
    row = lax.broadcasted_iota(jnp.int32, u.shape, 0)
    shifted = jnp.where(row == 0, prev_ref[...], pltpu.roll(u, 1, axis=0))
    prev_ref[...] = u[rows - 1:rows, :]
    us = u + (shifted - u) * mu_ref[...]

    r = us[:, 0:D_MAIN]
    k = us[:, D_MAIN:2 * D_MAIN]
    v = us[:, 2 * D_MAIN:3 * D_MAIN]
    x_wa = us[:, 3 * D_MAIN:3 * D_MAIN + D_LORA_WA]
    x_g = us[:, 3 * D_MAIN + D_LORA_WA:]

    w_log = -_softplus(-(w0_ref[...] + _dot(jnp.tanh(x_wa), wup_ref[...]))) - 0.5
    log_w = -jnp.exp(w_log)
    lr = _sigmoid(a0_ref[...] + _dot(x_wa, aup_ref[...]))
    gate = _dot(_sigmoid(x_g), gup_ref[...])
    kk_raw = k * kk_ref[...]
    k = k * (1.0 + (lr - 1.0) * ka_ref[...])

    tok_i = lax.broadcasted_iota(jnp.int32, (rows, rows), 0)
    tok_j = lax.broadcasted_iota(jnp.int32, (rows, rows), 1)
    shift = L.bit_length() - 1
    same_chunk_tril = jnp.logical_and(tok_i >= tok_j, (tok_i >> shift) == (tok_j >> shift))
    cum = _dot_f32(same_chunk_tril.astype(F32), log_w)
    cum_end = jnp.concatenate(
        [jnp.broadcast_to(cum[(ci + 1) * L - 1:(ci + 1) * L, :], (L, D_MAIN)) for ci in chunks], axis=0)
    g_incl = jnp.exp(cum)
    g_excl = jnp.exp(cum - log_w)
    g_inv = jnp.exp(-cum)
    g_tail = jnp.exp(cum_end - cum)

    lo = _lo_mask((rows, LANES))
    own = (lax.broadcasted_iota(jnp.int32, (L2, LANES), 0) < L) == _lo_mask((L2, LANES))
    tok_r = lax.broadcasted_iota(jnp.int32, (L2, L2), 0) & (L - 1)
    tok_c = lax.broadcasted_iota(jnp.int32, (L2, L2), 1) & (L - 1)
    strict = tok_r > tok_c
    incl = tok_r >= tok_c
    eye = (lax.broadcasted_iota(jnp.int32, (L2, L2), 0)
           == lax.broadcasted_iota(jnp.int32, (L2, L2), 1)).astype(F32)

    def stack(x):
        return jnp.where(own, jnp.concatenate([x, x], axis=0), 0.0)

    pairs = range(N_PAIRS)
    sls = [slice(p * LANES, (p + 1) * LANES) for p in pairs]
    units = [(ci, p) for ci in chunks for p in pairs]
    kkn, b = [], []
    for sl in sls:
        kk = kk_raw[:, sl]
        kkn.append(kk * lax.rsqrt(jnp.maximum(_pair_sum(kk * kk, lo), 1e-24)))
        b.append(kkn[-1] * lr[:, sl])
    a_dec = [-kkn[p] * g_excl[:, sls[p]] for p in pairs]
    r_dec = [r[:, sls[p]] * g_incl[:, sls[p]] for p in pairs]
    b_inv = [b[p] * g_inv[:, sls[p]] for p in pairs]
    k_inv = [k[:, sls[p]] * g_inv[:, sls[p]] for p in pairs]
    b_tail = [b[p] * g_tail[:, sls[p]] for p in pairs]
    k_tail = [k[:, sls[p]] * g_tail[:, sls[p]] for p in pairs]

    def chunk_rows(ci):
        return slice(ci * L, (ci + 1) * L)

    ar, v2, power, a_ak, a_r = {}, {}, {}, {}, {}
    for ci, p in units:
        rs = chunk_rows(ci)
        ar[ci, p] = jnp.concatenate([stack(a_dec[p][rs]), stack(r_dec[p][rs])],
                                    axis=0).astype(BF16)
        v2[ci, p] = stack(v[rs, sls[p]]).astype(BF16)
        bk_inv = jnp.concatenate([stack(b_inv[p][rs]), stack(k_inv[p][rs])], axis=0)
        g = _dot_nt(ar[ci, p], bk_inv)
        power[ci, p] = jnp.where(strict, g[:L2, :L2], 0.0)
        a_ak[ci, p] = jnp.where(strict, g[:L2, L2:], 0.0).astype(BF16)
        a_r[ci, p] = jnp.concatenate([jnp.where(incl, g[L2:, :L2], 0.0),
                                      jnp.where(incl, g[L2:, L2:], 0.0)],
                                     axis=1).astype(BF16)

    inv = {un: eye + power[un] for un in units}
    for _ in range(L.bit_length() - 2):
        power = {un: _dot(power[un], power[un]) for un in units}
        inv = {un: inv[un] + _dot(power[un], inv[un]) for un in units}
    from_v = {un: _dot(a_ak[un], v2[un]) for un in units}

    states = [state_ref[p] for p in pairs]
    y2 = {}
    for ci in chunks:
        rs = chunk_rows(ci)
        from_state = [_dot_nt(ar[ci, p], states[p]) for p in pairs]
        uv = [jnp.concatenate([_dot(inv[ci, p], from_state[p][:L2] + from_v[ci, p]).astype(BF16),
                               v2[ci, p]], axis=0) for p in pairs]
        for p in pairs:
            y2[ci, p] = from_state[p][L2:] + _dot(a_r[ci, p], uv[p])
        new_states = []
        for p in pairs:
            bk = jnp.concatenate([stack(b_tail[p][rs]), stack(k_tail[p][rs])], axis=0)
            decay = g_incl[(ci + 1) * L - 1:(ci + 1) * L, sls[p]]
            new_states.append(states[p] * decay + _dot_tn(uv[p], bk))
        states = new_states
    for p in pairs:
        state_ref[p] = states[p]

    for p in pairs:
        sl = sls[p]
        y = jnp.concatenate([y2[ci, p][:L] + y2[ci, p][L:] for ci in chunks], axis=0)
        mean = _pair_sum(y, lo) * (1.0 / HEAD_DIM)
        yc = y - mean
        var = _pair_sum(yc * yc, lo) * (1.0 / HEAD_DIM)
        yn = yc * lax.rsqrt(var + GN_EPS) * lng_ref[:, sl] + lnb_ref[:, sl]
        bonus = _pair_sum(r[:, sl] * k[:, sl] * rk_ref[:, sl], lo) * v[:, sl]
        o_ref[:, sl] = ((yn + bonus) * gate[:, sl]).astype(o_ref.dtype)


def _rwkv(u_main, bsz, mu, w_up, a_up, g_up, w0, a0, k_k, k_a, r_k, lnx_g, lnx_b):
    n, width = u_main.shape
    step_rows = RWKV_CHUNK * RWKV_STEP_CHUNKS
    nc = n // bsz // step_rows
    rows = [w0, a0, k_k, k_a, r_k.reshape(-1), lnx_g, lnx_b]
    zeros_wa = jnp.zeros((D_DECAY_LORA, D_MAIN), F32)
    wup = jnp.concatenate([w_up, zeros_wa], axis=0)
    aup = jnp.concatenate([zeros_wa, a_up], axis=0)
    gup = jnp.concatenate([g_up, jnp.zeros((D_LORA_G - D_GATE_LORA, D_MAIN), F32)], axis=0)
    return pl.pallas_call(
        _rwkv_kernel,
        out_shape=jax.ShapeDtypeStruct((n, D_MAIN), BF16),
        grid=(bsz, nc),
        in_specs=[pl.BlockSpec((step_rows, width), lambda b, c: (b * nc + c, 0)),
                  _const_spec((1, width)),
                  _const_spec(wup.shape), _const_spec(aup.shape), _const_spec(gup.shape)]
                 + [_const_spec((1, D_MAIN))] * len(rows),
        out_specs=pl.BlockSpec((step_rows, D_MAIN), lambda b, c: (b * nc + c, 0)),
        scratch_shapes=[pltpu.VMEM((N_PAIRS, LANES, LANES), F32),
                        pltpu.VMEM((1, width), F32)],
        compiler_params=_params("parallel", "arbitrary"),
        name="rwkv7",
    )(u_main, mu[None, :], wup, aup, gup, *[x[None, :] for x in rows])


def _split3(x):
    hi = x.astype(BF16).astype(F32)
    mid = (x - hi).astype(BF16).astype(F32)
    lo = (x - hi - mid).astype(BF16).astype(F32)
    return hi, mid, lo


def _bias_lanes(own, lane, base, ones_first, terms):
    one_at = base if ones_first else base + 3
    term_at = base + 3 if ones_first else base
    out = jnp.where(jnp.logical_and(lane >= one_at, lane < one_at + 3), 1.0, 0.0)
    for i, term in enumerate(terms):
        out = jnp.where(lane == term_at + i, term, out)
    return jnp.where(own, 0.0, out)


def _shared_kv_kernel(x_ref, g_ref, w_ref, wvt_ref, bf_ref, kg_ref, k_ref, vt_ref, c_ref,
                      carry_ref):
    @pl.when(pl.program_id(1) == 0)
    def _():
        carry_ref[...] = jnp.zeros_like(carry_ref)

    tm = x_ref.shape[0]
    h = _rms(x_ref[...], g_ref[...]).astype(BF16)
    vt_ref[0] = _dot_nt(wvt_ref[...], h).astype(BF16)
    logits = jnp.dot(h, w_ref[:, D_MAIN:], preferred_element_type=F32) + bf_ref[...]
    log_f = -_softplus(-logits)
    tri = _tril(CUMSUM_BLOCK, strict=False).astype(F32)
    carry = carry_ref[...]
    for j in range(0, tm, CUMSUM_BLOCK):
        cj = _dot_f32(tri, log_f[j:j + CUMSUM_BLOCK]) + carry
        c_ref[j:j + CUMSUM_BLOCK, :] = cj * LOG2E
        carry = cj[CUMSUM_BLOCK - 1:CUMSUM_BLOCK, :]
    carry_ref[...] = carry

    c2 = c_ref[...]
    lane = lax.broadcasted_iota(jnp.int32, (tm, LANES), 1)
    lo = lane < HEAD_DIM
    for p in range(N_PAIRS):
        sl = slice(p * LANES, (p + 1) * LANES)
        kn = _head_rms(jnp.dot(h, w_ref[:, sl], preferred_element_type=F32), kg_ref[:, sl])
        for j in range(2):
            head = 2 * p + j
            own = lo if j == 0 else jnp.logical_not(lo)
            ck = jnp.sum(jnp.where(lane == head, c2, 0.0), axis=-1, keepdims=True)
            bias = _bias_lanes(own, lane, HEAD_DIM * (1 - j), False, [-t for t in _split3(ck)])
            k_ref[:, head * LANES:(head + 1) * LANES] = jnp.where(own, kn, bias).astype(BF16)


def _shared_kv(x, bsz, kv_norm, w_kv, b_f, k_gain):
    n, d = x.shape
    tm = ROW_BLOCK
    nt = n // bsz // tm
    pad = LANES - N_MAIN_HEADS
    w = jnp.concatenate([w_kv[:, :D_MAIN], jnp.pad(w_kv[:, 2 * D_MAIN:], ((0, 0), (0, pad)))],
                        axis=1).astype(BF16)
    w_vt = w_kv[:, D_MAIN:2 * D_MAIN].T.astype(BF16)
    row = lambda b, i: (b * nt + i, 0)
    return pl.pallas_call(
        _shared_kv_kernel,
        out_shape=[jax.ShapeDtypeStruct((n, N_MAIN_HEADS * LANES), BF16),
                   jax.ShapeDtypeStruct((bsz, D_MAIN, n // bsz), BF16),
                   jax.ShapeDtypeStruct((n, LANES), F32)],
        grid=(bsz, nt),
        in_specs=[pl.BlockSpec((tm, d), row), _const_spec((1, d)), _const_spec(w.shape),
                  _const_spec(w_vt.shape), _const_spec((1, LANES)), _const_spec((1, D_MAIN))],
        out_specs=[pl.BlockSpec((tm, N_MAIN_HEADS * LANES), row),
                   pl.BlockSpec((1, D_MAIN, tm), lambda b, i: (b, 0, i)),
                   pl.BlockSpec((tm, LANES), row)],
        scratch_shapes=[pltpu.VMEM((1, LANES), F32)],
        compiler_params=_params("parallel", "arbitrary"),
        name="shared_kv",
    )(x, kv_norm[None, :], w, w_vt, jnp.pad(b_f, (0, pad))[None, :],
      jnp.tile(k_gain, N_MAIN_HEADS)[None, :])


def _fox_kernel(q_ref, k0_ref, k1_ref, vt_ref, c_ref, cend_ref, gate_ref, qg_ref, thr_ref, o_ref,
                sa_ref, sb_ref, p_ref, acc_ref):
    tq = q_ref.shape[1]
    p = pl.program_id(1)
    qi = pl.program_id(2)
    c_ends = cend_ref[0]
    head_lane = lax.broadcasted_iota(jnp.int32, c_ends.shape, 1) >> 1
    keep = jnp.where(jnp.logical_and(head_lane == p, c_ref[0, 0:1, :] - c_ends >= thr_ref[...]),
                     1.0, 0.0)
    skippable = jnp.sum(keep, axis=1, keepdims=True) == 0.0
    first = jnp.sum(jnp.where(skippable, 1, 0)).astype(jnp.int32)
    qn = _head_rms(q_ref[0], qg_ref[...]) * (ATTN_SCALE * LOG2E)
    lane = lax.broadcasted_iota(jnp.int32, qn.shape, 1)
    lo = lane < HEAD_DIM
    c2 = c_ref[0]
    k_refs = (k0_ref, k1_ref)
    q_tiles = []
    for j in range(2):
        own = lo if j == 0 else jnp.logical_not(lo)
        cq = jnp.sum(jnp.where(lane == 2 * p + j, c2, 0.0), axis=-1, keepdims=True)
        bias = _bias_lanes(own, lane, HEAD_DIM * (1 - j), True, _split3(cq))
        q_tiles.append(jnp.where(own, qn, bias).astype(BF16))
    acc_ref[...] = jnp.zeros_like(acc_ref)
    tk = sa_ref.shape[1]
    key_idx = lax.broadcasted_iota(jnp.int32, (tk, LANES), 0)
    qry_idx = lax.broadcasted_iota(jnp.int32, (tk, LANES), 1)

    def scores(block, s_ref):
        start = pl.multiple_of(block * tk, tk)
        for j in range(2):
            s_ref[j] = _dot_nt(k_refs[j][0, pl.ds(start, tk), :], q_tiles[j])

    def step(block, s_ref, slot, carry, key_offset):
        vtb = vt_ref[0, :, pl.ds(pl.multiple_of(block * tk, tk), tk)]
        vtb = jnp.concatenate([vtb, jnp.ones((SUM_ROWS, tk), BF16)], axis=0)
        out = []
        for j in range(2):
            m_old = carry[j]
            m_parts, a_parts = [], []
            for g in range(0, tq, LANES):
                gs = slice(g, g + LANES)
                s = s_ref[j, :, gs]
                if key_offset is not None:
                    s = jnp.where(key_idx + key_offset <= qry_idx + g, s, NEG_INF)
                m_new = jnp.maximum(m_old[:, gs], jnp.max(s, axis=0, keepdims=True))
                m_parts.append(m_new)
                a_parts.append(jnp.exp2(m_old[:, gs] - m_new))
                p_ref[slot, j, :, gs] = jnp.exp2((s - m_new).astype(BF16))
            alpha = jnp.concatenate(a_parts, axis=1)
            acc_ref[j] = acc_ref[j] * alpha + jnp.dot(vtb, p_ref[slot, j],
                                                      preferred_element_type=F32)
            out.append(jnp.concatenate(m_parts, axis=1))
        return tuple(out)

    def body(t, carry):
        scores(2 * t + 1, sb_ref)
        carry = step(2 * t, sa_ref, 0, carry, None)
        scores(2 * t + 2, sa_ref)
        return step(2 * t + 1, sb_ref, 1, carry, None)

    init = jnp.full((1, tq), NEG_INF, F32)
    scores(2 * first, sa_ref)
    carry = lax.fori_loop(first, qi, body, (init, init))
    scores(2 * qi + 1, sb_ref)
    carry = step(2 * qi, sa_ref, 0, carry, 0)
    step(2 * qi + 1, sb_ref, 1, carry, tk)
    heads = [(acc_ref[j, :LANES, :] / acc_ref[j, LANES:LANES + 1, :]).T for j in range(2)]
    o = jnp.where(lo, heads[0], heads[1])
    o_ref[0] = (o * _sigmoid(gate_ref[0])).astype(o_ref.dtype)


def _fox(q, gate, k_tiles, v_t, c2, q_gain, k_gain, bsz):
    n, _ = q.shape
    t = n // bsz
    tq = ATTN_BLOCK
    c2 = c2.reshape(bsz, t, LANES)
    c_ends = c2.reshape(bsz, t // tq, tq, LANES)[:, :, tq - 1, :]
    qk_bound = (1.01 * HEAD_DIM * ATTN_SCALE * LOG2E) * jnp.max(jnp.abs(q_gain)) * jnp.max(jnp.abs(k_gain))
    thr = jnp.full((1, LANES), -(2.0 * qk_bound + UNDERFLOW_LOG2), F32)
    blk = pl.BlockSpec((1, tq, LANES), lambda b, p, i: (b, i, p))
    out = pl.pallas_call(
        _fox_kernel,
        out_shape=jax.ShapeDtypeStruct((bsz, t, D_MAIN), BF16),
        grid=(bsz, N_PAIRS, t // tq),
        in_specs=[blk,
                  pl.BlockSpec((1, t, LANES), lambda b, p, i: (b, 0, 2 * p)),
                  pl.BlockSpec((1, t, LANES), lambda b, p, i: (b, 0, 2 * p + 1)),
                  pl.BlockSpec((1, LANES, t), lambda b, p, i: (b, p, 0)),
                  pl.BlockSpec((1, tq, LANES), lambda b, p, i: (b, i, 0)),
                  pl.BlockSpec((1, t // tq, LANES), lambda b, p, i: (b, 0, 0)),
                  blk, _const_spec((1, LANES)), _const_spec((1, LANES))],
        out_specs=blk,
        scratch_shapes=[pltpu.VMEM((2, tq // 2, tq), F32), pltpu.VMEM((2, tq // 2, tq), F32),
                        pltpu.VMEM((2, 2, tq // 2, tq), BF16),
                        pltpu.VMEM((2, LANES + SUM_ROWS, tq), F32)],
        compiler_params=_params("parallel", "parallel", "arbitrary"),
        name="fox_attn",
    )(q.reshape(bsz, t, D_MAIN), k_tiles.reshape(bsz, t, N_MAIN_HEADS * LANES),
      k_tiles.reshape(bsz, t, N_MAIN_HEADS * LANES), v_t,
      c2, c_ends, gate.reshape(bsz, t, D_MAIN), jnp.tile(q_gain, 2)[None, :], thr)
    return out.reshape(n, D_MAIN)


def _merge_ffn_kernel(x_ref, main_ref, memo_ref, wo_ref, g_ref, wgu_ref, wd_ref, o_ref):
    d_ff = wd_ref.shape[0]
    mixed = (jnp.dot(main_ref[...], wo_ref[:D_MAIN, :], preferred_element_type=F32)
             + jnp.dot(memo_ref[...], wo_ref[D_MAIN:, :], preferred_element_type=F32))
    x = x_ref[...] + mixed
    h = _rms(x, g_ref[...]).astype(BF16)
    acc = x
    for j in range(0, d_ff, FF_CHUNK):
        gj = jnp.dot(h, wgu_ref[:, j:j + FF_CHUNK], preferred_element_type=F32)
        uj = jnp.dot(h, wgu_ref[:, d_ff + j:d_ff + j + FF_CHUNK], preferred_element_type=F32)
        act = (gj * _sigmoid(gj) * uj).astype(BF16)
        acc = acc + jnp.dot(act, wd_ref[j:j + FF_CHUNK, :], preferred_element_type=F32)
    o_ref[...] = acc


def _merge_ffn(x, main, memo, w_out, ffn_norm, w_gate_up, w_down):
    n, d = x.shape
    tm = ROW_BLOCK
    row = lambda i: (i, 0)
    single = pl.Buffered(1)
    return pl.pallas_call(
        _merge_ffn_kernel,
        out_shape=jax.ShapeDtypeStruct((n, d), F32),
        grid=(n // tm,),
        in_specs=[pl.BlockSpec((tm, d), row), pl.BlockSpec((tm, D_MAIN), row),
                  pl.BlockSpec((tm, D_MEMH), row),
                  pl.BlockSpec(w_out.shape, lambda i: (0, 0), pipeline_mode=single),
                  _const_spec((1, d)),
                  pl.BlockSpec(w_gate_up.shape, lambda i: (0, 0), pipeline_mode=single),
                  pl.BlockSpec(w_down.shape, lambda i: (0, 0), pipeline_mode=single)],
        out_specs=pl.BlockSpec((tm, d), row),
        compiler_params=_params("parallel"),
        name="merge_ffn",
    )(x, main, memo, w_out.astype(BF16), ffn_norm[None, :], w_gate_up.astype(BF16),
      w_down.astype(BF16))


def _a_in_weights(w_in, mu):
    d_shift = 3 * D_MAIN + D_DECAY_LORA + D_AAA_LORA + D_GATE_LORA
    pad = D_LORA_G - D_GATE_LORA
    w = jnp.concatenate([w_in[:, :d_shift], jnp.zeros((w_in.shape[0], pad), w_in.dtype),
                         w_in[:, d_shift:]], axis=1)
    return w.astype(BF16), jnp.pad(mu, (0, pad))


def kernel(x, mem, mix_norm, w_out, mem_norm, w_mem_kv, mem_q_gain, mem_k_gain, ffn_norm, w_gate_up, w_down, a_w_in, a_mu, a_w0, a_w_up, a_a0, a_a_up, a_g_up, a_k_k, a_k_a, a_r_k, a_lnx_g, a_lnx_b, kv_norm, w_kv, b_f, k_gain, b_w_in, b_q_gain):
    bsz, t, d = x.shape
    n_a = a_w_in.shape[0]
    n_b = b_w_in.shape[0]
    x = x.reshape(bsz * t, d)
    k_mem, v_mem = _mem_kv(mem, mem_norm, w_mem_kv, mem_k_gain)

    def merge(x, l, main, mem_q):
        memo = _mem_attn(mem_q, k_mem, v_mem, mem_q_gain[l], l, bsz)
        return _merge_ffn(x, main, memo, w_out[l], ffn_norm[l], w_gate_up[l], w_down[l])

    for i in range(n_a):
        w, mu = _a_in_weights(a_w_in[i], a_mu[i])
        u_main, mem_q = _norm_proj(x, mix_norm[i][None, :], w, (D_A_MAIN, D_MEMH), (F32, F32))
        main = _rwkv(u_main, bsz, mu, a_w_up[i], a_a_up[i], a_g_up[i], a_w0[i], a_a0[i],
                     a_k_k[i], a_k_a[i], a_r_k[i], a_lnx_g[i], a_lnx_b[i])
        x = merge(x, i, main, mem_q)

    k_sh, v_sh, c_sh = _shared_kv(x, bsz, kv_norm, w_kv, b_f, k_gain)

    for j in range(n_b):
        l = n_a + j
        q, gate, mem_q = _norm_proj(x, mix_norm[l][None, :], b_w_in[j].astype(BF16),
                                    (D_MAIN, D_MAIN, D_MEMH), (F32, F32, F32))
        main = _fox(q, gate, k_sh, v_sh, c_sh, b_q_gain[j], k_gain, bsz)
        x = merge(x, l, main, mem_q)

    return x.reshape(bsz, t, d)
```

```python
import functools

import jax
import jax.numpy as jnp
from jax import lax
from jax.experimental import pallas as pl
from jax.experimental.pallas import tpu as pltpu

F32 = jnp.float32
BF16 = jnp.bfloat16
HIGHEST = lax.Precision.HIGHEST

HEAD_DIM = 64
LANES = 128
N_MAIN_HEADS = 12
N_PAIRS = N_MAIN_HEADS // 2
D_MAIN = N_MAIN_HEADS * HEAD_DIM
N_MEM_HEADS = 4
D_MEMH = N_MEM_HEADS * HEAD_DIM
D_DECAY_LORA = 64
D_AAA_LORA = 64
D_GATE_LORA = 160
D_LORA_WA = LANES
D_LORA_G = 2 * LANES
D_A_MAIN = 3 * D_MAIN + D_LORA_WA + D_LORA_G
RMS_EPS = 1e-6
GN_EPS = 64e-5
ATTN_SCALE = HEAD_DIM ** -0.5
NEG_INF = -1e30
LOG2E = 1.4426950408889634
VMEM_LIMIT = 56 * 1024 * 1024

RWKV_CHUNK = 64
RWKV_STEP_CHUNKS = 4
RWKV_GROUP_CHUNKS = 4
ROW_BLOCK = 512
ATTN_BLOCK = 512
CUMSUM_BLOCK = 128
SUM_ROWS = 16
UNDERFLOW_LOG2 = 160.0
FF_CHUNK = 1408


def _params(*sem):
    return pltpu.CompilerParams(dimension_semantics=sem, vmem_limit_bytes=VMEM_LIMIT)


def _const_spec(shape):
    nd = len(shape)
    return pl.BlockSpec(shape, lambda *_: (0,) * nd)


def _dot(a, b):
    return jnp.dot(a.astype(BF16), b.astype(BF16), preferred_element_type=F32)


def _dot_nt(a, b):
    return lax.dot_general(a.astype(BF16), b.astype(BF16), (((1,), (1,)), ((), ())),
                           preferred_element_type=F32)


def _dot_tn(a, b):
    return lax.dot_general(a.astype(BF16), b.astype(BF16), (((0,), (0,)), ((), ())),
                           preferred_element_type=F32)


def _dot_f32(a, b):
    return jnp.dot(a, b, preferred_element_type=F32, precision=HIGHEST)


def _sigmoid(x):
    return 1.0 / (1.0 + jnp.exp(-x))


def _softplus(x):
    return jnp.maximum(x, 0.0) + jnp.log(1.0 + jnp.exp(-jnp.abs(x)))


def _rms(x, g):
    return x * lax.rsqrt(jnp.mean(x * x, axis=-1, keepdims=True) + RMS_EPS) * g


def _lo_mask(shape):
    return lax.broadcasted_iota(jnp.int32, shape, len(shape) - 1) < HEAD_DIM


def _pair_sum(x, lo):
    s_lo = jnp.sum(jnp.where(lo, x, 0.0), axis=-1, keepdims=True)
    s_hi = jnp.sum(jnp.where(lo, 0.0, x), axis=-1, keepdims=True)
    return jnp.where(lo, s_lo, s_hi)


def _head_rms(x, gain):
    outs = []
    for j in range(x.shape[-1] // LANES):
        xb = x[:, j * LANES:(j + 1) * LANES]
        lo = _lo_mask(xb.shape)
        ms = _pair_sum(xb * xb, lo) * (1.0 / HEAD_DIM)
        outs.append(xb * lax.rsqrt(ms + RMS_EPS))
    y = outs[0] if len(outs) == 1 else jnp.concatenate(outs, axis=-1)
    return y * gain


def _tril(n, strict):
    r = lax.broadcasted_iota(jnp.int32, (n, n), 0)
    c = lax.broadcasted_iota(jnp.int32, (n, n), 1)
    return (r > c) if strict else (r >= c)


def _norm_proj_kernel(x_ref, g_ref, w_ref, *o_refs, splits):
    h = _rms(x_ref[...], g_ref[...]).astype(BF16)
    c0 = 0
    for o_ref, width in zip(o_refs, splits):
        for j in range(0, width, 2 * LANES):
            wj = min(2 * LANES, width - j)
            o_ref[:, j:j + wj] = jnp.dot(h, w_ref[:, c0 + j:c0 + j + wj],
                                         preferred_element_type=F32).astype(o_ref.dtype)
        c0 += width


def _norm_proj(x, g, w, splits, dtypes):
    n, d = x.shape
    tm = ROW_BLOCK
    return pl.pallas_call(
        functools.partial(_norm_proj_kernel, splits=splits),
        out_shape=[jax.ShapeDtypeStruct((n, s), dt) for s, dt in zip(splits, dtypes)],
        grid=(n // tm,),
        in_specs=[pl.BlockSpec((tm, d), lambda i: (i, 0)),
                  _const_spec((1, d)),
                  _const_spec(w.shape)],
        out_specs=[pl.BlockSpec((tm, s), lambda i: (i, 0)) for s in splits],
        compiler_params=_params("parallel"),
        name="norm_proj",
    )(x, g, w)


def _mem_kv_kernel(mem_ref, g_ref, w_ref, kg_ref, k_ref, v_ref):
    h = _rms(mem_ref[0], g_ref[0]).astype(BF16)
    kv = jnp.dot(h, w_ref[0], preferred_element_type=F32)
    k_ref[0, 0] = _head_rms(kv[:, :D_MEMH], kg_ref[0]).astype(BF16)
    v_ref[0, 0] = kv[:, D_MEMH:].astype(BF16)


def _mem_kv(mem, mem_norm, w_mem_kv, mem_k_gain):
    b, m, d = mem.shape
    depth = w_mem_kv.shape[0]
    out = jax.ShapeDtypeStruct((depth, b, m, D_MEMH), BF16)
    return pl.pallas_call(
        _mem_kv_kernel,
        out_shape=[out, out],
        grid=(depth, b),
        in_specs=[pl.BlockSpec((1, m, d), lambda l, i: (i, 0, 0)),
                  pl.BlockSpec((1, 1, d), lambda l, i: (l, 0, 0)),
                  pl.BlockSpec((1, d, 2 * D_MEMH), lambda l, i: (l, 0, 0)),
                  pl.BlockSpec((1, 1, D_MEMH), lambda l, i: (l, 0, 0))],
        out_specs=[pl.BlockSpec((1, 1, m, D_MEMH), lambda l, i: (l, i, 0, 0))] * 2,
        compiler_params=_params("parallel", "parallel"),
        name="mem_kv",
    )(mem, mem_norm[:, None, :], w_mem_kv.astype(BF16),
      jnp.tile(mem_k_gain, (1, N_MEM_HEADS))[:, None, :])


def _mem_attention(q, k, v, q_gain):
    qn = _head_rms(q, q_gain) * ATTN_SCALE
    lane = lax.broadcasted_iota(jnp.int32, qn.shape, 1)
    out = jnp.zeros(qn.shape, F32)
    for h in range(N_MEM_HEADS):
        sel = jnp.logical_and(lane >= h * HEAD_DIM, lane < (h + 1) * HEAD_DIM)
        s = _dot_nt(jnp.where(sel, qn, 0.0), k)
        e = jnp.exp(s - jnp.max(s, axis=-1, keepdims=True))
        o = _dot(e, v) / jnp.sum(e, axis=-1, keepdims=True)
        out = jnp.where(sel, o, out)
    return out


def _rwkv_kernel(u_ref, mu_ref, wup_ref, aup_ref, gup_ref, w0_ref, a0_ref, kk_ref, ka_ref,
                 rk_ref, lng_ref, lnb_ref, o_ref, state_ref, prev_ref):
    L = RWKV_CHUNK
    L2 = 2 * L
    assert L2 == LANES
    n_chunks = u_ref.shape[0] // L
    c = pl.program_id(1)

    @pl.when(c == 0)
    def _():
        state_ref[...] = jnp.zeros_like(state_ref)
        prev_ref[...] = jnp.zeros_like(prev_ref)

    own = (lax.broadcasted_iota(jnp.int32, (L2, LANES), 0) < L) == _lo_mask((L2, LANES))
    tok_r = lax.broadcasted_iota(jnp.int32, (L2, L2), 0) & (L - 1)
    tok_c = lax.broadcasted_iota(jnp.int32, (L2, L2), 1) & (L - 1)
    strict = tok_r > tok_c
    incl = tok_r >= tok_c
    eye = (lax.broadcasted_iota(jnp.int32, (L2, L2), 0)
           == lax.broadcasted_iota(jnp.int32, (L2, L2), 1)).astype(F32)
    pairs = range(N_PAIRS)
    sls = [slice(p * LANES, (p + 1) * LANES) for p in pairs]

    def stack(x):
        return jnp.where(own, jnp.concatenate([x, x], axis=0), 0.0)

    def front(first_chunk, chunks):
        rows = len(chunks) * L
        row0 = first_chunk * L
        u = u_ref[row0:row0 + rows, :]
        before = prev_ref[...] if first_chunk == 0 else u_ref[row0 - 1:row0, :]
        row = lax.broadcasted_iota(jnp.int32, u.shape, 0)
        us = u + (jnp.where(row == 0, before, pltpu.roll(u, 1, axis=0)) - u) * mu_ref[...]

        r = us[:, 0:D_MAIN]
        k = us[:, D_MAIN:2 * D_MAIN]
        v = us[:, 2 * D_MAIN:3 * D_MAIN]
        x_wa = us[:, 3 * D_MAIN:3 * D_MAIN + D_LORA_WA]
        x_g = us[:, 3 * D_MAIN + D_LORA_WA:]

        w_log = -_softplus(-(w0_ref[...] + _dot(jnp.tanh(x_wa), wup_ref[...]))) - 0.5
        log_w = -jnp.exp(w_log)
        lr = _sigmoid(a0_ref[...] + _dot(x_wa, aup_ref[...]))
        gate = _dot(_sigmoid(x_g), gup_ref[...])
        kk_raw = k * kk_ref[...]
        k = k * (1.0 + (lr - 1.0) * ka_ref[...])

        tok_i = lax.broadcasted_iota(jnp.int32, (rows, rows), 0)
        tok_j = lax.broadcasted_iota(jnp.int32, (rows, rows), 1)
        shift = L.bit_length() - 1
        same_chunk_tril = jnp.logical_and(tok_i >= tok_j, (tok_i >> shift) == (tok_j >> shift))
        cum = _dot_f32(same_chunk_tril.astype(F32), log_w)
        cum_end = jnp.concatenate(
            [jnp.broadcast_to(cum[(i + 1) * L - 1:(i + 1) * L, :], (L, D_MAIN))
             for i in range(len(chunks))], axis=0)
        g_incl = jnp.exp(cum)
        g_excl = jnp.exp(cum - log_w)
        g_inv = jnp.exp(-cum)
        g_tail = jnp.exp(cum_end - cum)

        lo = _lo_mask((rows, LANES))
        out = {"r": r, "k": k, "v": v, "gate": gate, "lo": lo}
        for p in pairs:
            sl = sls[p]
            kk = kk_raw[:, sl]
            kkn = kk * lax.rsqrt(jnp.maximum(_pair_sum(kk * kk, lo), 1e-24))
            b = kkn * lr[:, sl]
            a_dec = -kkn * g_excl[:, sl]
            r_dec = r[:, sl] * g_incl[:, sl]
            b_inv = b * g_inv[:, sl]
            k_inv = k[:, sl] * g_inv[:, sl]
            b_tail = b * g_tail[:, sl]
            k_tail = k[:, sl] * g_tail[:, sl]
            for i, ci in enumerate(chunks):
                rs = slice(i * L, (i + 1) * L)
                ar = jnp.concatenate([stack(a_dec[rs]), stack(r_dec[rs])],
                                     axis=0).astype(BF16)
                v2 = stack(v[rs, sl]).astype(BF16)
                bk_inv = jnp.concatenate([stack(b_inv[rs]), stack(k_inv[rs])], axis=0)
                g = _dot_nt(ar, bk_inv)
                a_r = jnp.concatenate([jnp.where(incl, g[L2:, :L2], 0.0),
                                       jnp.where(incl, g[L2:, L2:], 0.0)],
                                      axis=1).astype(BF16)
                out[ci, p] = {
                    "ar": ar, "v2": v2, "a_r": a_r,
                    "power": jnp.where(strict, g[:L2, :L2], 0.0),
                    "a_ak": jnp.where(strict, g[:L2, L2:], 0.0).astype(BF16),
                    "bk_tail": jnp.concatenate([stack(b_tail[rs]), stack(k_tail[rs])], axis=0),
                    "decay": g_incl[(i + 1) * L - 1:(i + 1) * L, sl]}
        return out

    def invert(fr, chunks):
        units = [(ci, p) for ci in chunks for p in pairs]
        power = {un: fr[un]["power"] for un in units}
        inv = {un: eye + power[un] for un in units}
        for _ in range(L.bit_length() - 2):
            power = {un: _dot(power[un], power[un]) for un in units}
            inv = {un: inv[un] + _dot(power[un], inv[un]) for un in units}
        for un in units:
            fr[un]["inv"] = inv[un]
            fr[un]["from_v"] = _dot(fr[un]["a_ak"], fr[un]["v2"])

    def advance(fr, chunks, states):
        ys = {p: [] for p in pairs}
        for ci in chunks:
            un = [fr[ci, p] for p in pairs]
            from_state = [_dot_nt(un[p]["ar"], states[p]) for p in pairs]
            uv = [jnp.concatenate([_dot(un[p]["inv"], from_state[p][:L2] + un[p]["from_v"]
                                        ).astype(BF16), un[p]["v2"]], axis=0) for p in pairs]
            for p in pairs:
                y2 = from_state[p][L2:] + _dot(un[p]["a_r"], uv[p])
                ys[p].append(y2[:L] + y2[L:])
            states = [states[p] * un[p]["decay"] + _dot_tn(uv[p], un[p]["bk_tail"]) for p in pairs]
        return ys, states

    def finish(fr, first_chunk, chunks, ys):
        row0 = first_chunk * L
        rows = len(chunks) * L
        lo = fr["lo"]
        for p in pairs:
            sl = sls[p]
            y = jnp.concatenate(ys[p], axis=0)
            mean = _pair_sum(y, lo) * (1.0 / HEAD_DIM)
            yc = y - mean
            var = _pair_sum(yc * yc, lo) * (1.0 / HEAD_DIM)
            yn = yc * lax.rsqrt(var + GN_EPS) * lng_ref[:, sl] + lnb_ref[:, sl]
            bonus = _pair_sum(fr["r"][:, sl] * fr["k"][:, sl] * rk_ref[:, sl], lo) * fr["v"][:, sl]
            o_ref[row0:row0 + rows, sl] = ((yn + bonus) * fr["gate"][:, sl]).astype(o_ref.dtype)

    groups = [list(range(g, min(g + RWKV_GROUP_CHUNKS, n_chunks)))
              for g in range(0, n_chunks, RWKV_GROUP_CHUNKS)]
    last_row = u_ref[n_chunks * L - 1:n_chunks * L, :]
    states = [state_ref[p] for p in pairs]
    fronts = [front(groups[0][0], groups[0])]
    for gi, chunks in enumerate(groups):
        invert(fronts[gi], chunks)
        if gi + 1 < len(groups):
            fronts.append(front(groups[gi + 1][0], groups[gi + 1]))
        ys, states = advance(fronts[gi], chunks, states)
        finish(fronts[gi], chunks[0], chunks, ys)
    prev_ref[...] = last_row
    for p in pairs:
        state_ref[p] = states[p]


def _rwkv(u_main, bsz, mu, w_up, a_up, g_up, w0, a0, k_k, k_a, r_k, lnx_g, lnx_b):
    n, width = u_main.shape
    step_rows = RWKV_CHUNK * RWKV_STEP_CHUNKS
    nc = n // bsz // step_rows
    rows = [w0, a0, k_k, k_a, r_k.reshape(-1), lnx_g, lnx_b]
    zeros_wa = jnp.zeros((D_DECAY_LORA, D_MAIN), F32)
    wup = jnp.concatenate([w_up, zeros_wa], axis=0)
    aup = jnp.concatenate([zeros_wa, a_up], axis=0)
    gup = jnp.concatenate([g_up, jnp.zeros((D_LORA_G - D_GATE_LORA, D_MAIN), F32)], axis=0)
    return pl.pallas_call(
        _rwkv_kernel,
        out_shape=jax.ShapeDtypeStruct((n, D_MAIN), BF16),
        grid=(bsz, nc),
        in_specs=[pl.BlockSpec((step_rows, width), lambda b, c: (b * nc + c, 0)),
                  _const_spec((1, width)),
                  _const_spec(wup.shape), _const_spec(aup.shape), _const_spec(gup.shape)]
                 + [_const_spec((1, D_MAIN))] * len(rows),
        out_specs=pl.BlockSpec((step_rows, D_MAIN), lambda b, c: (b * nc + c, 0)),
        scratch_shapes=[pltpu.VMEM((N_PAIRS, LANES, LANES), F32),
                        pltpu.VMEM((1, width), F32)],
        compiler_params=_params("parallel", "arbitrary"),
        name="rwkv7",
    )(u_main, mu[None, :], wup, aup, gup, *[x[None, :] for x in rows])


def _split3(x):
    hi = x.astype(BF16).astype(F32)
    mid = (x - hi).astype(BF16).astype(F32)
    lo = (x - hi - mid).astype(BF16).astype(F32)
    return hi, mid, lo


def _bias_lanes(own, lane, base, ones_first, terms):
    one_at = base if ones_first else base + 3
    term_at = base + 3 if ones_first else base
    out = jnp.where(jnp.logical_and(lane >= one_at, lane < one_at + 3), 1.0, 0.0)
    for i, term in enumerate(terms):
        out = jnp.where(lane == term_at + i, term, out)
    return jnp.where(own, 0.0, out)


def _shared_kv_kernel(x_ref, g_ref, w_ref, wvt_ref, bf_ref, kg_ref, k_ref, vt_ref, c_ref,
                      carry_ref):
    @pl.when(pl.program_id(1) == 0)
    def _():
        carry_ref[...] = jnp.zeros_like(carry_ref)

    tm = x_ref.shape[0]
    h = _rms(x_ref[...], g_ref[...]).astype(BF16)
    vt_ref[0] = _dot_nt(wvt_ref[...], h).astype(BF16)
    logits = jnp.dot(h, w_ref[:, D_MAIN:], preferred_element_type=F32) + bf_ref[...]
    log_f = -_softplus(-logits)
    tri = _tril(CUMSUM_BLOCK, strict=False).astype(F32)
    carry = carry_ref[...]
    for j in range(0, tm, CUMSUM_BLOCK):
        cj = _dot_f32(tri, log_f[j:j + CUMSUM_BLOCK]) + carry
        c_ref[j:j + CUMSUM_BLOCK, :] = cj * LOG2E
        carry = cj[CUMSUM_BLOCK - 1:CUMSUM_BLOCK, :]
    carry_ref[...] = carry

    c2 = c_ref[...]
    lane = lax.broadcasted_iota(jnp.int32, (tm, LANES), 1)
    lo = lane < HEAD_DIM
    for p in range(N_PAIRS):
        sl = slice(p * LANES, (p + 1) * LANES)
        kn = _head_rms(jnp.dot(h, w_ref[:, sl], preferred_element_type=F32), kg_ref[:, sl])
        for j in range(2):
            head = 2 * p + j
            own = lo if j == 0 else jnp.logical_not(lo)
            ck = jnp.sum(jnp.where(lane == head, c2, 0.0), axis=-1, keepdims=True)
            bias = _bias_lanes(own, lane, HEAD_DIM * (1 - j), False, [-t for t in _split3(ck)])
            k_ref[:, head * LANES:(head + 1) * LANES] = jnp.where(own, kn, bias).astype(BF16)


def _shared_kv(x, bsz, kv_norm, w_kv, b_f, k_gain):
    n, d = x.shape
    tm = ROW_BLOCK
    nt = n // bsz // tm
    pad = LANES - N_MAIN_HEADS
    w = jnp.concatenate([w_kv[:, :D_MAIN], jnp.pad(w_kv[:, 2 * D_MAIN:], ((0, 0), (0, pad)))],
                        axis=1).astype(BF16)
    w_vt = w_kv[:, D_MAIN:2 * D_MAIN].T.astype(BF16)
    row = lambda b, i: (b * nt + i, 0)
    return pl.pallas_call(
        _shared_kv_kernel,
        out_shape=[jax.ShapeDtypeStruct((n, N_MAIN_HEADS * LANES), BF16),
                   jax.ShapeDtypeStruct((bsz, D_MAIN, n // bsz), BF16),
                   jax.ShapeDtypeStruct((n, LANES), F32)],
        grid=(bsz, nt),
        in_specs=[pl.BlockSpec((tm, d), row), _const_spec((1, d)), _const_spec(w.shape),
                  _const_spec(w_vt.shape), _const_spec((1, LANES)), _const_spec((1, D_MAIN))],
        out_specs=[pl.BlockSpec((tm, N_MAIN_HEADS * LANES), row),
                   pl.BlockSpec((1, D_MAIN, tm), lambda b, i: (b, 0, i)),
                   pl.BlockSpec((tm, LANES), row)],
        scratch_shapes=[pltpu.VMEM((1, LANES), F32)],
        compiler_params=_params("parallel", "arbitrary"),
        name="shared_kv",
    )(x, kv_norm[None, :], w, w_vt, jnp.pad(b_f, (0, pad))[None, :],
      jnp.tile(k_gain, N_MAIN_HEADS)[None, :])


def _fox_kernel(q_ref, k0_ref, k1_ref, vt_ref, c_ref, cend_ref, gate_ref, qg_ref, thr_ref, o_ref,
                sa_ref, sb_ref, p_ref, acc_ref):
    tq = q_ref.shape[1]
    p = pl.program_id(1)
    qi = pl.program_id(2)
    c_ends = cend_ref[0]
    head_lane = lax.broadcasted_iota(jnp.int32, c_ends.shape, 1) >> 1
    keep = jnp.where(jnp.logical_and(head_lane == p, c_ref[0, 0:1, :] - c_ends >= thr_ref[...]),
                     1.0, 0.0)
    skippable = jnp.sum(keep, axis=1, keepdims=True) == 0.0
    first = jnp.sum(jnp.where(skippable, 1, 0)).astype(jnp.int32)
    qn = _head_rms(q_ref[0], qg_ref[...]) * (ATTN_SCALE * LOG2E)
    lane = lax.broadcasted_iota(jnp.int32, qn.shape, 1)
    lo = lane < HEAD_DIM
    c2 = c_ref[0]
    k_refs = (k0_ref, k1_ref)
    q_tiles = []
    for j in range(2):
        own = lo if j == 0 else jnp.logical_not(lo)
        cq = jnp.sum(jnp.where(lane == 2 * p + j, c2, 0.0), axis=-1, keepdims=True)
        bias = _bias_lanes(own, lane, HEAD_DIM * (1 - j), True, _split3(cq))
        q_tiles.append(jnp.where(own, qn, bias).astype(BF16))
    acc_ref[...] = jnp.zeros_like(acc_ref)
    tk = sa_ref.shape[1]
    key_idx = lax.broadcasted_iota(jnp.int32, (tk, LANES), 0)
    qry_idx = lax.broadcasted_iota(jnp.int32, (tk, LANES), 1)

    def scores(block, s_ref):
        start = pl.multiple_of(block * tk, tk)
        for j in range(2):
            s_ref[j] = _dot_nt(k_refs[j][0, pl.ds(start, tk), :], q_tiles[j])

    def step(block, s_ref, slot, carry, key_offset):
        vtb = vt_ref[0, :, pl.ds(pl.multiple_of(block * tk, tk), tk)]
        vtb = jnp.concatenate([vtb, jnp.ones((SUM_ROWS, tk), BF16)], axis=0)
        out = []
        for j in range(2):
            m_old = carry[j]
            m_parts, a_parts = [], []
            for g in range(0, tq, LANES):
                gs = slice(g, g + LANES)
                s = s_ref[j, :, gs]
                if key_offset is not None:
                    s = jnp.where(key_idx + key_offset <= qry_idx + g, s, NEG_INF)
                m_new = jnp.maximum(m_old[:, gs], jnp.max(s, axis=0, keepdims=True))
                m_parts.append(m_new)
                a_parts.append(jnp.exp2(m_old[:, gs] - m_new))
                p_ref[slot, j, :, gs] = jnp.exp2((s - m_new).astype(BF16))
            alpha = jnp.concatenate(a_parts, axis=1)
            acc_ref[j] = acc_ref[j] * alpha + jnp.dot(vtb, p_ref[slot, j],
                                                      preferred_element_type=F32)
            out.append(jnp.concatenate(m_parts, axis=1))
        return tuple(out)

    def body(t, carry):
        scores(2 * t + 1, sb_ref)
        carry = step(2 * t, sa_ref, 0, carry, None)
        scores(2 * t + 2, sa_ref)
        return step(2 * t + 1, sb_ref, 1, carry, None)

    init = jnp.full((1, tq), NEG_INF, F32)
    scores(2 * first, sa_ref)
    carry = lax.fori_loop(first, qi, body, (init, init))
    scores(2 * qi + 1, sb_ref)
    carry = step(2 * qi, sa_ref, 0, carry, 0)
    step(2 * qi + 1, sb_ref, 1, carry, tk)
    heads = [(acc_ref[j, :LANES, :] / acc_ref[j, LANES:LANES + 1, :]).T for j in range(2)]
    o = jnp.where(lo, heads[0], heads[1])
    o_ref[0] = (o * _sigmoid(gate_ref[0])).astype(o_ref.dtype)


def _fox(q, gate, k_tiles, v_t, c2, q_gain, k_gain, bsz):
    n, _ = q.shape
    t = n // bsz
    tq = ATTN_BLOCK
    c2 = c2.reshape(bsz, t, LANES)
    c_ends = c2.reshape(bsz, t // tq, tq, LANES)[:, :, tq - 1, :]
    qk_bound = (1.01 * HEAD_DIM * ATTN_SCALE * LOG2E) * jnp.max(jnp.abs(q_gain)) * jnp.max(jnp.abs(k_gain))
    thr = jnp.full((1, LANES), -(2.0 * qk_bound + UNDERFLOW_LOG2), F32)
    blk = pl.BlockSpec((1, tq, LANES), lambda b, p, i: (b, i, p))
    out = pl.pallas_call(
        _fox_kernel,
        out_shape=jax.ShapeDtypeStruct((bsz, t, D_MAIN), BF16),
        grid=(bsz, N_PAIRS, t // tq),
        in_specs=[blk,
                  pl.BlockSpec((1, t, LANES), lambda b, p, i: (b, 0, 2 * p)),
                  pl.BlockSpec((1, t, LANES), lambda b, p, i: (b, 0, 2 * p + 1)),
                  pl.BlockSpec((1, LANES, t), lambda b, p, i: (b, p, 0)),
                  pl.BlockSpec((1, tq, LANES), lambda b, p, i: (b, i, 0)),
                  pl.BlockSpec((1, t // tq, LANES), lambda b, p, i: (b, 0, 0)),
                  blk, _const_spec((1, LANES)), _const_spec((1, LANES))],
        out_specs=blk,
        scratch_shapes=[pltpu.VMEM((2, tq // 2, tq), F32), pltpu.VMEM((2, tq // 2, tq), F32),
                        pltpu.VMEM((2, 2, tq // 2, tq), BF16),
                        pltpu.VMEM((2, LANES + SUM_ROWS, tq), F32)],
        compiler_params=_params("parallel", "parallel", "arbitrary"),
        name="fox_attn",
    )(q.reshape(bsz, t, D_MAIN), k_tiles.reshape(bsz, t, N_MAIN_HEADS * LANES),
      k_tiles.reshape(bsz, t, N_MAIN_HEADS * LANES), v_t,
      c2, c_ends, gate.reshape(bsz, t, D_MAIN), jnp.tile(q_gain, 2)[None, :], thr)
    return out.reshape(n, D_MAIN)


def _merge_ffn_kernel(x_ref, main_ref, mq_ref, mk_ref, mv_ref, mqg_ref, wo_ref, g_ref, wgu_ref,
                      wd_ref, o_ref):
    d_ff = wd_ref.shape[0]
    memo = _mem_attention(mq_ref[...], mk_ref[0, 0], mv_ref[0, 0], mqg_ref[...]).astype(BF16)
    mixed = (jnp.dot(main_ref[...], wo_ref[:D_MAIN, :], preferred_element_type=F32)
             + jnp.dot(memo, wo_ref[D_MAIN:, :], preferred_element_type=F32))
    x = x_ref[...] + mixed
    h = _rms(x, g_ref[...]).astype(BF16)
    acc = x
    for j in range(0, d_ff, FF_CHUNK):
        gj = jnp.dot(h, wgu_ref[:, j:j + FF_CHUNK], preferred_element_type=F32)
        uj = jnp.dot(h, wgu_ref[:, d_ff + j:d_ff + j + FF_CHUNK], preferred_element_type=F32)
        act = (gj * _sigmoid(gj) * uj).astype(BF16)
        acc = acc + jnp.dot(act, wd_ref[j:j + FF_CHUNK, :], preferred_element_type=F32)
    o_ref[...] = acc


def _merge_ffn(x, main, mem_q, k_mem, v_mem, mem_q_gain, layer, bsz, w_out, ffn_norm, w_gate_up,
               w_down):
    n, d = x.shape
    tm = ROW_BLOCK
    blocks_per_batch = n // bsz // tm
    row = lambda i: (i, 0)
    single = pl.Buffered(1)
    mem_spec = pl.BlockSpec((1, 1) + k_mem.shape[2:], lambda i: (layer, i // blocks_per_batch, 0, 0))
    return pl.pallas_call(
        _merge_ffn_kernel,
        out_shape=jax.ShapeDtypeStruct((n, d), F32),
        grid=(n // tm,),
        in_specs=[pl.BlockSpec((tm, d), row), pl.BlockSpec((tm, D_MAIN), row),
                  pl.BlockSpec((tm, D_MEMH), row), mem_spec, mem_spec, _const_spec((1, D_MEMH)),
                  pl.BlockSpec(w_out.shape, lambda i: (0, 0), pipeline_mode=single),
                  _const_spec((1, d)),
                  pl.BlockSpec(w_gate_up.shape, lambda i: (0, 0), pipeline_mode=single),
                  pl.BlockSpec(w_down.shape, lambda i: (0, 0), pipeline_mode=single)],
        out_specs=pl.BlockSpec((tm, d), row),
        compiler_params=_params("parallel"),
        name="merge_ffn",
    )(x, main, mem_q, k_mem, v_mem, jnp.tile(mem_q_gain, N_MEM_HEADS)[None, :],
      w_out.astype(BF16), ffn_norm[None, :], w_gate_up.astype(BF16), w_down.astype(BF16))


def _a_in_weights(w_in, mu):
    d_shift = 3 * D_MAIN + D_DECAY_LORA + D_AAA_LORA + D_GATE_LORA
    pad = D_LORA_G - D_GATE_LORA
    w = jnp.concatenate([w_in[:, :d_shift], jnp.zeros((w_in.shape[0], pad), w_in.dtype),
                         w_in[:, d_shift:]], axis=1)
    return w.astype(BF16), jnp.pad(mu, (0, pad))


def kernel(x, mem, mix_norm, w_out, mem_norm, w_mem_kv, mem_q_gain, mem_k_gain, ffn_norm, w_gate_up, w_down, a_w_in, a_mu, a_w0, a_w_up, a_a0, a_a_up, a_g_up, a_k_k, a_k_a, a_r_k, a_lnx_g, a_lnx_b, kv_norm, w_kv, b_f, k_gain, b_w_in, b_q_gain):
    bsz, t, d = x.shape
    n_a = a_w_in.shape[0]
    n_b = b_w_in.shape[0]
    x = x.reshape(bsz * t, d)
    k_mem, v_mem = _mem_kv(mem, mem_norm, w_mem_kv, mem_k_gain)

    def merge(x, l, main, mem_q):
        return _merge_ffn(x, main, mem_q, k_mem, v_mem, mem_q_gain[l], l, bsz, w_out[l],
                          ffn_norm[l], w_gate_up[l], w_down[l])

    for i in range(n_a):
        w, mu = _a_in_weights(a_w_in[i], a_mu[i])
        u_main, mem_q = _norm_proj(x, mix_norm[i][None, :], w, (D_A_MAIN, D_MEMH), (F32, F32))
        main = _rwkv(u_main, bsz, mu, a_w_up[i], a_a_up[i], a_g_up[i], a_w0[i], a_a0[i],
                     a_k_k[i], a_k_a[i], a_r_k[i], a_lnx_g[i], a_lnx_b[i])
        x = merge(x, i, main, mem_q)

    k_sh, v_sh, c_sh = _shared_kv(x, bsz, kv_norm, w_kv, b_f, k_gain)

    for j in range(n_b):
        l = n_a + j
        q, gate, mem_q = _norm_proj(x, mix_norm[l][None, :], b_w_in[j].astype(BF16),
                                    (D_MAIN, D_MAIN, D_MEMH), (F32, F32, F32))
        main = _fox(q, gate, k_sh, v_sh, c_sh, b_q_gain[j], k_gain, bsz)
        x = merge(x, l, main, mem_q)

    return x.reshape(bsz, t, d)
```

```python
import functools

import jax
import jax.numpy as jnp
from jax import lax
from jax.experimental import pallas as pl
from jax.experimental.pallas import tpu as pltpu

F32 = jnp.float32
BF16 = jnp.bfloat16
HIGHEST = lax.Precision.HIGHEST

HEAD_DIM = 64
LANES = 128
N_MAIN_HEADS = 12
N_PAIRS = N_MAIN_HEADS // 2
D_MAIN = N_MAIN_HEADS * HEAD_DIM
N_MEM_HEADS = 4
D_MEMH = N_MEM_HEADS * HEAD_DIM
D_DECAY_LORA = 64
D_AAA_LORA = 64
D_GATE_LORA = 160
D_LORA_WA = LANES
D_LORA_G = 2 * LANES
D_A_MAIN = 3 * D_MAIN + D_LORA_WA + D_LORA_G
RMS_EPS = 1e-6
GN_EPS = 64e-5
ATTN_SCALE = HEAD_DIM ** -0.5
NEG_INF = -1e30
LOG2E = 1.4426950408889634
VMEM_LIMIT = 56 * 1024 * 1024

RWKV_CHUNK = 64
RWKV_STEP_CHUNKS = 4
RWKV_GROUP_CHUNKS = 4
ROW_BLOCK = 512
ATTN_BLOCK = 512
CUMSUM_BLOCK = 128
SUM_ROWS = 16
UNDERFLOW_LOG2 = 160.0
FF_CHUNK = 1408


def _params(*sem):
    return pltpu.CompilerParams(dimension_semantics=sem, vmem_limit_bytes=VMEM_LIMIT)


def _const_spec(shape):
    nd = len(shape)
    return pl.BlockSpec(shape, lambda *_: (0,) * nd)


def _dot(a, b):
    return jnp.dot(a.astype(BF16), b.astype(BF16), preferred_element_type=F32)


def _dot_nt(a, b):
    return lax.dot_general(a.astype(BF16), b.astype(BF16), (((1,), (1,)), ((), ())),
                           preferred_element_type=F32)


def _dot_tn(a, b):
    return lax.dot_general(a.astype(BF16), b.astype(BF16), (((0,), (0,)), ((), ())),
                           preferred_element_type=F32)


def _dot_f32(a, b):
    return jnp.dot(a, b, preferred_element_type=F32, precision=HIGHEST)


def _sigmoid(x):
    return 1.0 / (1.0 + jnp.exp(-x))


def _softplus(x):
    return jnp.maximum(x, 0.0) + jnp.log(1.0 + jnp.exp(-jnp.abs(x)))


def _rms(x, g):
    return x * lax.rsqrt(jnp.mean(x * x, axis=-1, keepdims=True) + RMS_EPS) * g


def _lo_mask(shape):
    return lax.broadcasted_iota(jnp.int32, shape, len(shape) - 1) < HEAD_DIM


def _pair_sum(x, lo):
    s_lo = jnp.sum(jnp.where(lo, x, 0.0), axis=-1, keepdims=True)
    s_hi = jnp.sum(jnp.where(lo, 0.0, x), axis=-1, keepdims=True)
    return jnp.where(lo, s_lo, s_hi)


def _head_rms(x, gain):
    outs = []
    for j in range(x.shape[-1] // LANES):
        xb = x[:, j * LANES:(j + 1) * LANES]
        lo = _lo_mask(xb.shape)
        ms = _pair_sum(xb * xb, lo) * (1.0 / HEAD_DIM)
        outs.append(xb * lax.rsqrt(ms + RMS_EPS))
    y = outs[0] if len(outs) == 1 else jnp.concatenate(outs, axis=-1)
    return y * gain


def _tril(n, strict):
    r = lax.broadcasted_iota(jnp.int32, (n, n), 0)
    c = lax.broadcasted_iota(jnp.int32, (n, n), 1)
    return (r > c) if strict else (r >= c)


def _norm_proj_kernel(x_ref, g_ref, w_ref, *o_refs, splits):
    h = _rms(x_ref[...], g_ref[...]).astype(BF16)
    c0 = 0
    for o_ref, width in zip(o_refs, splits):
        for j in range(0, width, 2 * LANES):
            wj = min(2 * LANES, width - j)
            o_ref[:, j:j + wj] = jnp.dot(h, w_ref[:, c0 + j:c0 + j + wj],
                                         preferred_element_type=F32).astype(o_ref.dtype)
        c0 += width


def _norm_proj(x, g, w, splits, dtypes):
    n, d = x.shape
    tm = ROW_BLOCK
    return pl.pallas_call(
        functools.partial(_norm_proj_kernel, splits=splits),
        out_shape=[jax.ShapeDtypeStruct((n, s), dt) for s, dt in zip(splits, dtypes)],
        grid=(n // tm,),
        in_specs=[pl.BlockSpec((tm, d), lambda i: (i, 0)),
                  _const_spec((1, d)),
                  _const_spec(w.shape)],
        out_specs=[pl.BlockSpec((tm, s), lambda i: (i, 0)) for s in splits],
        compiler_params=_params("parallel"),
        name="norm_proj",
    )(x, g, w)


def _mem_kv_kernel(mem_ref, g_ref, w_ref, kg_ref, k_ref, v_ref):
    h = _rms(mem_ref[0], g_ref[0]).astype(BF16)
    kv = jnp.dot(h, w_ref[0], preferred_element_type=F32)
    k_ref[0, 0] = _head_rms(kv[:, :D_MEMH], kg_ref[0]).astype(BF16)
    v_ref[0, 0] = kv[:, D_MEMH:].astype(BF16)


def _mem_kv(mem, mem_norm, w_mem_kv, mem_k_gain):
    b, m, d = mem.shape
    depth = w_mem_kv.shape[0]
    out = jax.ShapeDtypeStruct((depth, b, m, D_MEMH), BF16)
    return pl.pallas_call(
        _mem_kv_kernel,
        out_shape=[out, out],
        grid=(depth, b),
        in_specs=[pl.BlockSpec((1, m, d), lambda l, i: (i, 0, 0)),
                  pl.BlockSpec((1, 1, d), lambda l, i: (l, 0, 0)),
                  pl.BlockSpec((1, d, 2 * D_MEMH), lambda l, i: (l, 0, 0)),
                  pl.BlockSpec((1, 1, D_MEMH), lambda l, i: (l, 0, 0))],
        out_specs=[pl.BlockSpec((1, 1, m, D_MEMH), lambda l, i: (l, i, 0, 0))] * 2,
        compiler_params=_params("parallel", "parallel"),
        name="mem_kv",
    )(mem, mem_norm[:, None, :], w_mem_kv.astype(BF16),
      jnp.tile(mem_k_gain, (1, N_MEM_HEADS))[:, None, :])


def _mem_attention(q, k, v, q_gain):
    qn = _head_rms(q, q_gain) * ATTN_SCALE
    lane = lax.broadcasted_iota(jnp.int32, qn.shape, 1)
    out = jnp.zeros(qn.shape, F32)
    for h in range(N_MEM_HEADS):
        sel = jnp.logical_and(lane >= h * HEAD_DIM, lane < (h + 1) * HEAD_DIM)
        s = _dot_nt(jnp.where(sel, qn, 0.0), k)
        e = jnp.exp(s - jnp.max(s, axis=-1, keepdims=True))
        o = _dot(e, v) / jnp.sum(e, axis=-1, keepdims=True)
        out = jnp.where(sel, o, out)
    return out


def _rwkv_kernel(u_ref, mu_ref, wup_ref, aup_ref, gup_ref, w0_ref, a0_ref, kk_ref, ka_ref,
                 rk_ref, lng_ref, lnb_ref, o_ref, state_ref, prev_ref):
    L = RWKV_CHUNK
    L2 = 2 * L
    assert L2 == LANES
    n_chunks = u_ref.shape[0] // L
    c = pl.program_id(1)

    @pl.when(c == 0)
    def _():
        state_ref[...] = jnp.zeros_like(state_ref)
        prev_ref[...] = jnp.zeros_like(prev_ref)

    own = (lax.broadcasted_iota(jnp.int32, (L2, LANES), 0) < L) == _lo_mask((L2, LANES))
    tok_r = lax.broadcasted_iota(jnp.int32, (L2, L2), 0) & (L - 1)
    tok_c = lax.broadcasted_iota(jnp.int32, (L2, L2), 1) & (L - 1)
    strict = tok_r > tok_c
    incl = tok_r >= tok_c
    eye = (lax.broadcasted_iota(jnp.int32, (L2, L2), 0)
           == lax.broadcasted_iota(jnp.int32, (L2, L2), 1)).astype(F32)
    pairs = range(N_PAIRS)
    sls = [slice(p * LANES, (p + 1) * LANES) for p in pairs]

    def stack(x):
        return jnp.where(own, jnp.concatenate([x, x], axis=0), 0.0)

    def front(first_chunk, chunks):
        rows = len(chunks) * L
        row0 = first_chunk * L
        u = u_ref[row0:row0 + rows, :]
        before = prev_ref[...] if first_chunk == 0 else u_ref[row0 - 1:row0, :]
        row = lax.broadcasted_iota(jnp.int32, u.shape, 0)
        us = u + (jnp.where(row == 0, before, pltpu.roll(u, 1, axis=0)) - u) * mu_ref[...]

        r = us[:, 0:D_MAIN]
        k = us[:, D_MAIN:2 * D_MAIN]
        v = us[:, 2 * D_MAIN:3 * D_MAIN]
        x_wa = us[:, 3 * D_MAIN:3 * D_MAIN + D_LORA_WA]
        x_g = us[:, 3 * D_MAIN + D_LORA_WA:]

        w_log = -_softplus(-(w0_ref[...] + _dot(jnp.tanh(x_wa), wup_ref[...]))) - 0.5
        log_w = jnp.exp(w_log) * (-LOG2E)
        lr = _sigmoid(a0_ref[...] + _dot(x_wa, aup_ref[...]))
        gate = _dot(_sigmoid(x_g), gup_ref[...])
        kk_raw = k * kk_ref[...]
        k = k * (1.0 + (lr - 1.0) * ka_ref[...])

        tok_i = lax.broadcasted_iota(jnp.int32, (rows, rows), 0)
        tok_j = lax.broadcasted_iota(jnp.int32, (rows, rows), 1)
        shift = L.bit_length() - 1
        same_chunk_tril = jnp.logical_and(tok_i >= tok_j, (tok_i >> shift) == (tok_j >> shift))
        cum = _dot_f32(same_chunk_tril.astype(F32), log_w)
        g_incl = jnp.exp2(cum)
        g_excl = jnp.exp2(cum - log_w)
        g_inv = jnp.exp2(-cum)
        g_end = jnp.concatenate(
            [jnp.broadcast_to(g_incl[(i + 1) * L - 1:(i + 1) * L, :], (L, D_MAIN))
             for i in range(len(chunks))], axis=0)
        g_tail = g_end * g_inv

        lo = _lo_mask((rows, LANES))
        out = {"r": r, "k": k, "v": v, "gate": gate, "lo": lo}
        for p in pairs:
            sl = sls[p]
            kk = kk_raw[:, sl]
            kkn = kk * lax.rsqrt(jnp.maximum(_pair_sum(kk * kk, lo), 1e-24))
            b = kkn * lr[:, sl]
            a_dec = -kkn * g_excl[:, sl]
            r_dec = r[:, sl] * g_incl[:, sl]
            b_inv = b * g_inv[:, sl]
            k_inv = k[:, sl] * g_inv[:, sl]
            b_tail = b * g_tail[:, sl]
            k_tail = k[:, sl] * g_tail[:, sl]
            for i, ci in enumerate(chunks):
                rs = slice(i * L, (i + 1) * L)
                ar = jnp.concatenate([stack(a_dec[rs]), stack(r_dec[rs])],
                                     axis=0).astype(BF16)
                v2 = stack(v[rs, sl]).astype(BF16)
                bk_inv = jnp.concatenate([stack(b_inv[rs]), stack(k_inv[rs])], axis=0)
                g = _dot_nt(ar, bk_inv)
                a_r = jnp.concatenate([jnp.where(incl, g[L2:, :L2], 0.0),
                                       jnp.where(incl, g[L2:, L2:], 0.0)],
                                      axis=1).astype(BF16)
                out[ci, p] = {
                    "ar": ar, "v2": v2, "a_r": a_r,
                    "power": jnp.where(strict, g[:L2, :L2], 0.0),
                    "a_ak": jnp.where(strict, g[:L2, L2:], 0.0).astype(BF16),
                    "bk_tail": jnp.concatenate([stack(b_tail[rs]), stack(k_tail[rs])], axis=0),
                    "decay": g_incl[(i + 1) * L - 1:(i + 1) * L, sl]}
        return out

    def invert(fr, chunks):
        units = [(ci, p) for ci in chunks for p in pairs]
        power = {un: fr[un]["power"] for un in units}
        inv = {un: eye + power[un] for un in units}
        for _ in range(L.bit_length() - 2):
            power = {un: _dot(power[un], power[un]) for un in units}
            inv = {un: inv[un] + _dot(power[un], inv[un]) for un in units}
        for un in units:
            fr[un]["inv"] = inv[un]
            fr[un]["from_v"] = _dot(fr[un]["a_ak"], fr[un]["v2"])

    def advance(fr, chunks, states):
        ys = {p: [] for p in pairs}
        for ci in chunks:
            un = [fr[ci, p] for p in pairs]
            from_state = [_dot_nt(un[p]["ar"], states[p]) for p in pairs]
            uv = [jnp.concatenate([_dot(un[p]["inv"], from_state[p][:L2] + un[p]["from_v"]
                                        ).astype(BF16), un[p]["v2"]], axis=0) for p in pairs]
            for p in pairs:
                y2 = from_state[p][L2:] + _dot(un[p]["a_r"], uv[p])
                ys[p].append(y2[:L] + y2[L:])
            states = [states[p] * un[p]["decay"] + _dot_tn(uv[p], un[p]["bk_tail"]) for p in pairs]
        return ys, states

    def finish(fr, first_chunk, chunks, ys):
        row0 = first_chunk * L
        rows = len(chunks) * L
        lo = fr["lo"]
        for p in pairs:
            sl = sls[p]
            y = jnp.concatenate(ys[p], axis=0)
            mean = _pair_sum(y, lo) * (1.0 / HEAD_DIM)
            yc = y - mean
            var = _pair_sum(yc * yc, lo) * (1.0 / HEAD_DIM)
            yn = yc * lax.rsqrt(var + GN_EPS) * lng_ref[:, sl] + lnb_ref[:, sl]
            bonus = _pair_sum(fr["r"][:, sl] * fr["k"][:, sl] * rk_ref[:, sl], lo) * fr["v"][:, sl]
            o_ref[row0:row0 + rows, sl] = ((yn + bonus) * fr["gate"][:, sl]).astype(o_ref.dtype)

    groups = [list(range(g, min(g + RWKV_GROUP_CHUNKS, n_chunks)))
              for g in range(0, n_chunks, RWKV_GROUP_CHUNKS)]
    last_row = u_ref[n_chunks * L - 1:n_chunks * L, :]
    states = [state_ref[p] for p in pairs]
    fronts = [front(groups[0][0], groups[0])]
    for gi, chunks in enumerate(groups):
        invert(fronts[gi], chunks)
        if gi + 1 < len(groups):
            fronts.append(front(groups[gi + 1][0], groups[gi + 1]))
        ys, states = advance(fronts[gi], chunks, states)
        finish(fronts[gi], chunks[0], chunks, ys)
    prev_ref[...] = last_row
    for p in pairs:
        state_ref[p] = states[p]


def _rwkv(u_main, bsz, mu, w_up, a_up, g_up, w0, a0, k_k, k_a, r_k, lnx_g, lnx_b):
    n, width = u_main.shape
    step_rows = RWKV_CHUNK * RWKV_STEP_CHUNKS
    nc = n // bsz // step_rows
    rows = [w0, a0, k_k, k_a, r_k.reshape(-1), lnx_g, lnx_b]
    zeros_wa = jnp.zeros((D_DECAY_LORA, D_MAIN), F32)
    wup = jnp.concatenate([w_up, zeros_wa], axis=0)
    aup = jnp.concatenate([zeros_wa, a_up], axis=0)
    gup = jnp.concatenate([g_up, jnp.zeros((D_LORA_G - D_GATE_LORA, D_MAIN), F32)], axis=0)
    return pl.pallas_call(
        _rwkv_kernel,
        out_shape=jax.ShapeDtypeStruct((n, D_MAIN), BF16),
        grid=(bsz, nc),
        in_specs=[pl.BlockSpec((step_rows, width), lambda b, c: (b * nc + c, 0)),
                  _const_spec((1, width)),
                  _const_spec(wup.shape), _const_spec(aup.shape), _const_spec(gup.shape)]
                 + [_const_spec((1, D_MAIN))] * len(rows),
        out_specs=pl.BlockSpec((step_rows, D_MAIN), lambda b, c: (b * nc + c, 0)),
        scratch_shapes=[pltpu.VMEM((N_PAIRS, LANES, LANES), F32),
                        pltpu.VMEM((1, width), F32)],
        compiler_params=_params("parallel", "arbitrary"),
        name="rwkv7",
    )(u_main, mu[None, :], wup, aup, gup, *[x[None, :] for x in rows])


def _split3(x):
    hi = x.astype(BF16).astype(F32)
    mid = (x - hi).astype(BF16).astype(F32)
    lo = (x - hi - mid).astype(BF16).astype(F32)
    return hi, mid, lo


def _key_bias(c2, base):
    lane = lax.broadcasted_iota(jnp.int32, c2.shape, 1)
    terms = _split3(jnp.where(lane < N_MAIN_HEADS, -c2, 0.0))
    out = jnp.where(jnp.logical_and(lane >= base + 3 * N_MAIN_HEADS,
                                    lane < base + 3 * N_MAIN_HEADS + 3), 1.0, 0.0)
    for i, term in enumerate(terms):
        shift = base + i * N_MAIN_HEADS
        out = out + (pltpu.roll(term, shift, axis=1) if shift else term)
    return out


def _query_bias(lane, base, head, cq):
    out = jnp.zeros(lane.shape, F32)
    for i in range(3):
        out = jnp.where(lane == base + i * N_MAIN_HEADS + head, 1.0, out)
    for i, term in enumerate(_split3(cq)):
        out = jnp.where(lane == base + 3 * N_MAIN_HEADS + i, term, out)
    return out


def _shared_kv_kernel(x_ref, g_ref, w_ref, wvt_ref, bf_ref, kg_ref, k_ref, vt_ref, c_ref,
                      carry_ref):
    @pl.when(pl.program_id(1) == 0)
    def _():
        carry_ref[...] = jnp.zeros_like(carry_ref)

    tm = x_ref.shape[0]
    h = _rms(x_ref[...], g_ref[...]).astype(BF16)
    vt_ref[0] = _dot_nt(wvt_ref[...], h).astype(BF16)
    logits = jnp.dot(h, w_ref[:, D_MAIN:], preferred_element_type=F32) + bf_ref[...]
    log_f = -_softplus(-logits)
    tri = _tril(CUMSUM_BLOCK, strict=False).astype(F32)
    carry = carry_ref[...]
    for j in range(0, tm, CUMSUM_BLOCK):
        cj = _dot_f32(tri, log_f[j:j + CUMSUM_BLOCK]) + carry
        c_ref[j:j + CUMSUM_BLOCK, :] = cj * LOG2E
        carry = cj[CUMSUM_BLOCK - 1:CUMSUM_BLOCK, :]
    carry_ref[...] = carry

    c2 = c_ref[...]
    lo = _lo_mask((tm, LANES))
    bias = (_key_bias(c2, HEAD_DIM), _key_bias(c2, 0))
    for p in range(N_PAIRS):
        sl = slice(p * LANES, (p + 1) * LANES)
        kn = _head_rms(jnp.dot(h, w_ref[:, sl], preferred_element_type=F32), kg_ref[:, sl])
        for j in range(2):
            head = 2 * p + j
            own = lo if j == 0 else jnp.logical_not(lo)
            k_ref[:, head * LANES:(head + 1) * LANES] = jnp.where(own, kn, bias[j]).astype(BF16)


def _shared_kv(x, bsz, kv_norm, w_kv, b_f, k_gain):
    n, d = x.shape
    tm = ROW_BLOCK
    nt = n // bsz // tm
    pad = LANES - N_MAIN_HEADS
    w = jnp.concatenate([w_kv[:, :D_MAIN], jnp.pad(w_kv[:, 2 * D_MAIN:], ((0, 0), (0, pad)))],
                        axis=1).astype(BF16)
    w_vt = w_kv[:, D_MAIN:2 * D_MAIN].T.astype(BF16)
    row = lambda b, i: (b * nt + i, 0)
    return pl.pallas_call(
        _shared_kv_kernel,
        out_shape=[jax.ShapeDtypeStruct((n, N_MAIN_HEADS * LANES), BF16),
                   jax.ShapeDtypeStruct((bsz, D_MAIN, n // bsz), BF16),
                   jax.ShapeDtypeStruct((n, LANES), F32)],
        grid=(bsz, nt),
        in_specs=[pl.BlockSpec((tm, d), row), _const_spec((1, d)), _const_spec(w.shape),
                  _const_spec(w_vt.shape), _const_spec((1, LANES)), _const_spec((1, D_MAIN))],
        out_specs=[pl.BlockSpec((tm, N_MAIN_HEADS * LANES), row),
                   pl.BlockSpec((1, D_MAIN, tm), lambda b, i: (b, 0, i)),
                   pl.BlockSpec((tm, LANES), row)],
        scratch_shapes=[pltpu.VMEM((1, LANES), F32)],
        compiler_params=_params("parallel", "arbitrary"),
        name="shared_kv",
    )(x, kv_norm[None, :], w, w_vt, jnp.pad(b_f, (0, pad))[None, :],
      jnp.tile(k_gain, N_MAIN_HEADS)[None, :])


def _fox_kernel(q_ref, k0_ref, k1_ref, vt_ref, c_ref, cend_ref, gate_ref, qg_ref, thr_ref, o_ref,
                sa_ref, sb_ref, p_ref, acc_ref):
    tq = q_ref.shape[1]
    p = pl.program_id(1)
    qi = pl.program_id(2)
    c_ends = cend_ref[0]
    head_lane = lax.broadcasted_iota(jnp.int32, c_ends.shape, 1) >> 1
    keep = jnp.where(jnp.logical_and(head_lane == p, c_ref[0, 0:1, :] - c_ends >= thr_ref[...]),
                     1.0, 0.0)
    skippable = jnp.sum(keep, axis=1, keepdims=True) == 0.0
    first = jnp.sum(jnp.where(skippable, 1, 0)).astype(jnp.int32)
    qn = _head_rms(q_ref[0], qg_ref[...]) * (ATTN_SCALE * LOG2E)
    lane = lax.broadcasted_iota(jnp.int32, qn.shape, 1)
    lo = lane < HEAD_DIM
    c2 = c_ref[0]
    k_refs = (k0_ref, k1_ref)
    q_tiles = []
    for j in range(2):
        own = lo if j == 0 else jnp.logical_not(lo)
        cq = jnp.sum(jnp.where(lane == 2 * p + j, c2, 0.0), axis=-1, keepdims=True)
        bias = _query_bias(lane, HEAD_DIM * (1 - j), 2 * p + j, cq)
        q_tiles.append(jnp.where(own, qn, bias).astype(BF16))
    acc_ref[...] = jnp.zeros_like(acc_ref)
    tk = sa_ref.shape[1]
    key_idx = lax.broadcasted_iota(jnp.int32, (tk, LANES), 0)
    qry_idx = lax.broadcasted_iota(jnp.int32, (tk, LANES), 1)

    def scores(block, s_ref, q_lo=0):
        start = pl.multiple_of(block * tk, tk)
        for j in range(2):
            s_ref[j, :, q_lo:] = _dot_nt(k_refs[j][0, pl.ds(start, tk), :],
                                         q_tiles[j][q_lo:, :])

    def step(block, s_ref, slot, carry, key_offset, q_lo=0):
        keys = pl.ds(pl.multiple_of(block * tk, tk), tk)
        ones = jnp.ones((SUM_ROWS, tk), BF16)
        out = []
        for j in range(2):
            vtb = jnp.concatenate([vt_ref[0, j * HEAD_DIM:(j + 1) * HEAD_DIM, keys], ones], axis=0)
            m_old = carry[j]
            m_parts, a_parts = [m_old[:, :q_lo]] if q_lo else [], []
            for g in range(q_lo, tq, LANES):
                gs = slice(g, g + LANES)
                s = s_ref[j, :, gs]
                if key_offset is not None:
                    s = jnp.where(key_idx + key_offset <= qry_idx + g, s, NEG_INF)
                m_new = jnp.maximum(m_old[:, gs], jnp.max(s, axis=0, keepdims=True))
                m_parts.append(m_new)
                a_parts.append(jnp.exp2(m_old[:, gs] - m_new))
                p_ref[slot, j, :, gs] = jnp.exp2((s - m_new).astype(BF16))
            alpha = jnp.concatenate(a_parts, axis=1)
            acc_ref[j, :, q_lo:] = acc_ref[j, :, q_lo:] * alpha + jnp.dot(
                vtb, p_ref[slot, j, :, q_lo:], preferred_element_type=F32)
            out.append(jnp.concatenate(m_parts, axis=1))
        return tuple(out)

    def body(t, carry):
        scores(2 * t + 1, sb_ref)
        carry = step(2 * t, sa_ref, 0, carry, None)
        scores(2 * t + 2, sa_ref)
        return step(2 * t + 1, sb_ref, 1, carry, None)

    init = jnp.full((1, tq), NEG_INF, F32)
    scores(2 * first, sa_ref)
    carry = lax.fori_loop(first, qi, body, (init, init))
    scores(2 * qi + 1, sb_ref, tk)
    carry = step(2 * qi, sa_ref, 0, carry, 0)
    step(2 * qi + 1, sb_ref, 1, carry, tk, tk)
    o = jnp.concatenate([acc_ref[j, :HEAD_DIM, :] / acc_ref[j, HEAD_DIM:HEAD_DIM + 1, :]
                         for j in range(2)], axis=0).T
    o_ref[0] = (o * _sigmoid(gate_ref[0])).astype(o_ref.dtype)


def _fox(q, gate, k_tiles, v_t, c2, q_gain, k_gain, bsz):
    n, _ = q.shape
    t = n // bsz
    tq = ATTN_BLOCK
    c2 = c2.reshape(bsz, t, LANES)
    c_ends = c2.reshape(bsz, t // tq, tq, LANES)[:, :, tq - 1, :]
    qk_bound = (1.01 * HEAD_DIM * ATTN_SCALE * LOG2E) * jnp.max(jnp.abs(q_gain)) * jnp.max(jnp.abs(k_gain))
    thr = jnp.full((1, LANES), -(2.0 * qk_bound + UNDERFLOW_LOG2), F32)
    blk = pl.BlockSpec((1, tq, LANES), lambda b, p, i: (b, i, p))
    out = pl.pallas_call(
        _fox_kernel,
        out_shape=jax.ShapeDtypeStruct((bsz, t, D_MAIN), BF16),
        grid=(bsz, N_PAIRS, t // tq),
        in_specs=[blk,
                  pl.BlockSpec((1, t, LANES), lambda b, p, i: (b, 0, 2 * p)),
                  pl.BlockSpec((1, t, LANES), lambda b, p, i: (b, 0, 2 * p + 1)),
                  pl.BlockSpec((1, LANES, t), lambda b, p, i: (b, p, 0)),
                  pl.BlockSpec((1, tq, LANES), lambda b, p, i: (b, i, 0)),
                  pl.BlockSpec((1, t // tq, LANES), lambda b, p, i: (b, 0, 0)),
                  blk, _const_spec((1, LANES)), _const_spec((1, LANES))],
        out_specs=blk,
        scratch_shapes=[pltpu.VMEM((2, tq // 2, tq), F32), pltpu.VMEM((2, tq // 2, tq), F32),
                        pltpu.VMEM((2, 2, tq // 2, tq), BF16),
                        pltpu.VMEM((2, HEAD_DIM + SUM_ROWS, tq), F32)],
        compiler_params=_params("parallel", "parallel", "arbitrary"),
        name="fox_attn",
    )(q.reshape(bsz, t, D_MAIN), k_tiles.reshape(bsz, t, N_MAIN_HEADS * LANES),
      k_tiles.reshape(bsz, t, N_MAIN_HEADS * LANES), v_t,
      c2, c_ends, gate.reshape(bsz, t, D_MAIN), jnp.tile(q_gain, 2)[None, :], thr)
    return out.reshape(n, D_MAIN)


def _merge_ffn_kernel(x_ref, main_ref, mq_ref, mk_ref, mv_ref, mqg_ref, wo_ref, g_ref, wgu_ref,
                      wd_ref, o_ref):
    d_ff = wd_ref.shape[0]
    memo = _mem_attention(mq_ref[...], mk_ref[0, 0], mv_ref[0, 0], mqg_ref[...]).astype(BF16)
    mixed = (jnp.dot(main_ref[...], wo_ref[:D_MAIN, :], preferred_element_type=F32)
             + jnp.dot(memo, wo_ref[D_MAIN:, :], preferred_element_type=F32))
    x = x_ref[...] + mixed
    h = _rms(x, g_ref[...]).astype(BF16)
    acc = x
    for j in range(0, d_ff, FF_CHUNK):
        gj = jnp.dot(h, wgu_ref[:, j:j + FF_CHUNK], preferred_element_type=F32)
        uj = jnp.dot(h, wgu_ref[:, d_ff + j:d_ff + j + FF_CHUNK], preferred_element_type=F32)
        act = (gj * _sigmoid(gj) * uj).astype(BF16)
        acc = acc + jnp.dot(act, wd_ref[j:j + FF_CHUNK, :], preferred_element_type=F32)
    o_ref[...] = acc


def _merge_ffn(x, main, mem_q, k_mem, v_mem, mem_q_gain, layer, bsz, w_out, ffn_norm, w_gate_up,
               w_down):
    n, d = x.shape
    tm = ROW_BLOCK
    blocks_per_batch = n // bsz // tm
    row = lambda i: (i, 0)
    single = pl.Buffered(1)
    mem_spec = pl.BlockSpec((1, 1) + k_mem.shape[2:], lambda i: (layer, i // blocks_per_batch, 0, 0))
    return pl.pallas_call(
        _merge_ffn_kernel,
        out_shape=jax.ShapeDtypeStruct((n, d), F32),
        grid=(n // tm,),
        in_specs=[pl.BlockSpec((tm, d), row), pl.BlockSpec((tm, D_MAIN), row),
                  pl.BlockSpec((tm, D_MEMH), row), mem_spec, mem_spec, _const_spec((1, D_MEMH)),
                  pl.BlockSpec(w_out.shape, lambda i: (0, 0), pipeline_mode=single),
                  _const_spec((1, d)),
                  pl.BlockSpec(w_gate_up.shape, lambda i: (0, 0), pipeline_mode=single),
                  pl.BlockSpec(w_down.shape, lambda i: (0, 0), pipeline_mode=single)],
        out_specs=pl.BlockSpec((tm, d), row),
        compiler_params=_params("parallel"),
        name="merge_ffn",
    )(x, main, mem_q, k_mem, v_mem, jnp.tile(mem_q_gain, N_MEM_HEADS)[None, :],
      w_out.astype(BF16), ffn_norm[None, :], w_gate_up.astype(BF16), w_down.astype(BF16))


def _a_in_weights(w_in, mu):
    d_shift = 3 * D_MAIN + D_DECAY_LORA + D_AAA_LORA + D_GATE_LORA
    pad = D_LORA_G - D_GATE_LORA
    w = jnp.concatenate([w_in[:, :d_shift], jnp.zeros((w_in.shape[0], pad), w_in.dtype),
                         w_in[:, d_shift:]], axis=1)
    return w.astype(BF16), jnp.pad(mu, (0, pad))


def kernel(x, mem, mix_norm, w_out, mem_norm, w_mem_kv, mem_q_gain, mem_k_gain, ffn_norm, w_gate_up, w_down, a_w_in, a_mu, a_w0, a_w_up, a_a0, a_a_up, a_g_up, a_k_k, a_k_a, a_r_k, a_lnx_g, a_lnx_b, kv_norm, w_kv, b_f, k_gain, b_w_in, b_q_gain):
    bsz, t, d = x.shape
    n_a = a_w_in.shape[0]
    n_b = b_w_in.shape[0]
    x = x.reshape(bsz * t, d)
    k_mem, v_mem = _mem_kv(mem, mem_norm, w_mem_kv, mem_k_gain)

    def merge(x, l, main, mem_q):
        return _merge_ffn(x, main, mem_q, k_mem, v_mem, mem_q_gain[l], l, bsz, w_out[l],
                          ffn_norm[l], w_gate_up[l], w_down[l])

    for i in range(n_a):
        w, mu = _a_in_weights(a_w_in[i], a_mu[i])
        u_main, mem_q = _norm_proj(x, mix_norm[i][None, :], w, (D_A_MAIN, D_MEMH), (F32, F32))
        main = _rwkv(u_main, bsz, mu, a_w_up[i], a_a_up[i], a_g_up[i], a_w0[i], a_a0[i],
                     a_k_k[i], a_k_a[i], a_r_k[i], a_lnx_g[i], a_lnx_b[i])
        x = merge(x, i, main, mem_q)

    k_sh, v_sh, c_sh = _shared_kv(x, bsz, kv_norm, w_kv, b_f, k_gain)

    for j in range(n_b):
        l = n_a + j
        q, gate, mem_q = _norm_proj(x, mix_norm[l][None, :], b_w_in[j].astype(BF16),
                                    (D_MAIN, D_MAIN, D_MEMH), (F32, F32, F32))
        main = _fox(q, gate, k_sh, v_sh, c_sh, b_q_gain[j], k_gain, bsz)
        x = merge(x, l, main, mem_q)

    return x.reshape(bsz, t, d)
```

```python
import functools

import jax
import jax.numpy as jnp
from jax import lax
from jax.experimental import pallas as pl
from jax.experimental.pallas import tpu as pltpu

F32 = jnp.float32
BF16 = jnp.bfloat16
HIGHEST = lax.Precision.HIGHEST

HEAD_DIM = 64
LANES = 128
N_MAIN_HEADS = 12
N_PAIRS = N_MAIN_HEADS // 2
D_MAIN = N_MAIN_HEADS * HEAD_DIM
N_MEM_HEADS = 4
D_MEMH = N_MEM_HEADS * HEAD_DIM
D_DECAY_LORA = 64
D_AAA_LORA = 64
D_GATE_LORA = 160
D_LORA_WA = LANES
D_LORA_G = 2 * LANES
D_A_MAIN = 3 * D_MAIN + D_LORA_WA + D_LORA_G
RMS_EPS = 1e-6
GN_EPS = 64e-5
KK_NORM_FLOOR = 1e-24
ATTN_SCALE = HEAD_DIM ** -0.5
NEG_INF = -1e30
LOG2E = 1.4426950408889634
VMEM_LIMIT = 56 * 1024 * 1024

RWKV_CHUNK = 64
RWKV_STEP_CHUNKS = 4
ROW_BLOCK = 512
ATTN_BLOCK = 512
CUMSUM_BLOCK = 128
SUM_ROWS = 16
UNDERFLOW_LOG2 = 160.0
FF_CHUNK = 1408


def _params(*sem):
    return pltpu.CompilerParams(dimension_semantics=sem, vmem_limit_bytes=VMEM_LIMIT)


def _const_spec(shape):
    nd = len(shape)
    return pl.BlockSpec(shape, lambda *_: (0,) * nd)


def _dot(a, b):
    return jnp.dot(a.astype(BF16), b.astype(BF16), preferred_element_type=F32)


def _dot_nt(a, b):
    return lax.dot_general(a.astype(BF16), b.astype(BF16), (((1,), (1,)), ((), ())),
                           preferred_element_type=F32)


def _dot_tn(a, b):
    return lax.dot_general(a.astype(BF16), b.astype(BF16), (((0,), (0,)), ((), ())),
                           preferred_element_type=F32)


def _dot_f32(a, b):
    return jnp.dot(a, b, preferred_element_type=F32, precision=HIGHEST)


def _sigmoid(x):
    return 1.0 / (1.0 + jnp.exp(-x))


def _softplus(x):
    return jnp.maximum(x, 0.0) + jnp.log(1.0 + jnp.exp(-jnp.abs(x)))


def _rms(x, g):
    return x * lax.rsqrt(jnp.mean(x * x, axis=-1, keepdims=True) + RMS_EPS) * g


def _lo_mask(shape):
    return lax.broadcasted_iota(jnp.int32, shape, len(shape) - 1) < HEAD_DIM


def _pair_sum(x, lo):
    s_lo = jnp.sum(jnp.where(lo, x, 0.0), axis=-1, keepdims=True)
    s_hi = jnp.sum(jnp.where(lo, 0.0, x), axis=-1, keepdims=True)
    return jnp.where(lo, s_lo, s_hi)


def _head_rms(x, gain):
    outs = []
    for j in range(x.shape[-1] // LANES):
        xb = x[:, j * LANES:(j + 1) * LANES]
        lo = _lo_mask(xb.shape)
        ms = _pair_sum(xb * xb, lo) * (1.0 / HEAD_DIM)
        outs.append(xb * lax.rsqrt(ms + RMS_EPS))
    y = outs[0] if len(outs) == 1 else jnp.concatenate(outs, axis=-1)
    return y * gain


def _tril(n, strict):
    r = lax.broadcasted_iota(jnp.int32, (n, n), 0)
    c = lax.broadcasted_iota(jnp.int32, (n, n), 1)
    return (r > c) if strict else (r >= c)


def _norm_proj_kernel(x_ref, g_ref, w_ref, *o_refs, splits):
    h = _rms(x_ref[...], g_ref[...]).astype(BF16)
    c0 = 0
    for o_ref, width in zip(o_refs, splits):
        for j in range(0, width, 2 * LANES):
            wj = min(2 * LANES, width - j)
            o_ref[:, j:j + wj] = jnp.dot(h, w_ref[:, c0 + j:c0 + j + wj],
                                         preferred_element_type=F32).astype(o_ref.dtype)
        c0 += width


def _norm_proj(x, g, w, splits, dtypes):
    n, d = x.shape
    tm = ROW_BLOCK
    return pl.pallas_call(
        functools.partial(_norm_proj_kernel, splits=splits),
        out_shape=[jax.ShapeDtypeStruct((n, s), dt) for s, dt in zip(splits, dtypes)],
        grid=(n // tm,),
        in_specs=[pl.BlockSpec((tm, d), lambda i: (i, 0)),
                  _const_spec((1, d)),
                  _const_spec(w.shape)],
        out_specs=[pl.BlockSpec((tm, s), lambda i: (i, 0)) for s in splits],
        compiler_params=_params("parallel"),
        name="norm_proj",
    )(x, g, w)


def _mem_kv_kernel(mem_ref, g_ref, w_ref, kg_ref, k_ref, v_ref):
    h = _rms(mem_ref[0], g_ref[0]).astype(BF16)
    kv = jnp.dot(h, w_ref[0], preferred_element_type=F32)
    k_ref[0, 0] = _head_rms(kv[:, :D_MEMH], kg_ref[0]).astype(BF16)
    v_ref[0, 0] = kv[:, D_MEMH:].astype(BF16)


def _mem_kv(mem, mem_norm, w_mem_kv, mem_k_gain):
    b, m, d = mem.shape
    depth = w_mem_kv.shape[0]
    out = jax.ShapeDtypeStruct((depth, b, m, D_MEMH), BF16)
    return pl.pallas_call(
        _mem_kv_kernel,
        out_shape=[out, out],
        grid=(depth, b),
        in_specs=[pl.BlockSpec((1, m, d), lambda l, i: (i, 0, 0)),
                  pl.BlockSpec((1, 1, d), lambda l, i: (l, 0, 0)),
                  pl.BlockSpec((1, d, 2 * D_MEMH), lambda l, i: (l, 0, 0)),
                  pl.BlockSpec((1, 1, D_MEMH), lambda l, i: (l, 0, 0))],
        out_specs=[pl.BlockSpec((1, 1, m, D_MEMH), lambda l, i: (l, i, 0, 0))] * 2,
        compiler_params=_params("parallel", "parallel"),
        name="mem_kv",
    )(mem, mem_norm[:, None, :], w_mem_kv.astype(BF16),
      jnp.tile(mem_k_gain, (1, N_MEM_HEADS))[:, None, :])


def _mem_attention(q, k, v, q_gain):
    qn = _head_rms(q, q_gain) * ATTN_SCALE
    lane = lax.broadcasted_iota(jnp.int32, qn.shape, 1)
    out = jnp.zeros(qn.shape, F32)
    for h in range(N_MEM_HEADS):
        sel = jnp.logical_and(lane >= h * HEAD_DIM, lane < (h + 1) * HEAD_DIM)
        s = _dot_nt(jnp.where(sel, qn, 0.0), k)
        e = jnp.exp(s - jnp.max(s, axis=-1, keepdims=True))
        o = _dot(e, v) / jnp.sum(e, axis=-1, keepdims=True)
        out = jnp.where(sel, o, out)
    return out


def _rwkv_kernel(u_ref, mu_ref, wup_ref, aup_ref, gup_ref, w0_ref, a0_ref, kk_ref, ka_ref,
                 rk_ref, lng_ref, lnb_ref, o_ref, state_ref, prev_ref):
    L = RWKV_CHUNK
    L2 = 2 * L
    assert L2 == LANES
    n_chunks = u_ref.shape[0] // L
    c = pl.program_id(1)

    @pl.when(c == 0)
    def _():
        state_ref[...] = jnp.zeros_like(state_ref)
        prev_ref[...] = jnp.zeros_like(prev_ref)

    own = (lax.broadcasted_iota(jnp.int32, (L2, LANES), 0) < L) == _lo_mask((L2, LANES))
    tok_r = lax.broadcasted_iota(jnp.int32, (L2, L2), 0) & (L - 1)
    tok_c = lax.broadcasted_iota(jnp.int32, (L2, L2), 1) & (L - 1)
    strict = tok_r > tok_c
    incl = tok_r >= tok_c
    eye = (lax.broadcasted_iota(jnp.int32, (L2, L2), 0)
           == lax.broadcasted_iota(jnp.int32, (L2, L2), 1)).astype(F32)
    pairs = range(N_PAIRS)
    sls = [slice(p * LANES, (p + 1) * LANES) for p in pairs]

    def stack(x):
        return jnp.where(own, jnp.concatenate([x, x], axis=0), 0.0)

    def front(chunks):
        rows = len(chunks) * L
        u = u_ref[...]
        row = lax.broadcasted_iota(jnp.int32, u.shape, 0)
        us = u + (jnp.where(row == 0, prev_ref[...], pltpu.roll(u, 1, axis=0)) - u) * mu_ref[...]

        r = us[:, 0:D_MAIN]
        k = us[:, D_MAIN:2 * D_MAIN]
        v = us[:, 2 * D_MAIN:3 * D_MAIN]
        x_wa = us[:, 3 * D_MAIN:3 * D_MAIN + D_LORA_WA]
        x_g = us[:, 3 * D_MAIN + D_LORA_WA:]

        w_log = -_softplus(-(w0_ref[...] + _dot(jnp.tanh(x_wa), wup_ref[...]))) - 0.5
        log_w = jnp.exp(w_log) * (-LOG2E)
        lr = _sigmoid(a0_ref[...] + _dot(x_wa, aup_ref[...]))
        gate = _dot(_sigmoid(x_g), gup_ref[...])
        kk_raw = k * kk_ref[...]
        k = k * (1.0 + (lr - 1.0) * ka_ref[...])

        tok_i = lax.broadcasted_iota(jnp.int32, (rows, rows), 0)
        tok_j = lax.broadcasted_iota(jnp.int32, (rows, rows), 1)
        shift = L.bit_length() - 1
        same_chunk_tril = jnp.logical_and(tok_i >= tok_j, (tok_i >> shift) == (tok_j >> shift))
        cum = _dot_f32(same_chunk_tril.astype(F32), log_w)
        g_incl = jnp.exp2(cum)
        g_excl = jnp.exp2(cum - log_w)
        g_inv = jnp.exp2(-cum)
        g_end = jnp.concatenate(
            [jnp.broadcast_to(g_incl[(i + 1) * L - 1:(i + 1) * L, :], (L, D_MAIN))
             for i in range(len(chunks))], axis=0)
        g_tail = g_end * g_inv

        lo = _lo_mask((rows, LANES))
        out = {"r": r, "k": k, "v": v, "gate": gate, "lo": lo}
        for p in pairs:
            sl = sls[p]
            kk = kk_raw[:, sl]
            kkn = kk * lax.rsqrt(jnp.maximum(_pair_sum(kk * kk, lo), KK_NORM_FLOOR))
            b = kkn * lr[:, sl]
            a_dec = -kkn * g_excl[:, sl]
            r_dec = r[:, sl] * g_incl[:, sl]
            b_inv = b * g_inv[:, sl]
            k_inv = k[:, sl] * g_inv[:, sl]
            b_tail = b * g_tail[:, sl]
            k_tail = k[:, sl] * g_tail[:, sl]
            for i, ci in enumerate(chunks):
                rs = slice(i * L, (i + 1) * L)
                ar = jnp.concatenate([stack(a_dec[rs]), stack(r_dec[rs])],
                                     axis=0).astype(BF16)
                v2 = stack(v[rs, sl]).astype(BF16)
                bk_inv = jnp.concatenate([stack(b_inv[rs]), stack(k_inv[rs])], axis=0)
                g = _dot_nt(ar, bk_inv)
                a_r = jnp.concatenate([jnp.where(incl, g[L2:, :L2], 0.0),
                                       jnp.where(incl, g[L2:, L2:], 0.0)],
                                      axis=1).astype(BF16)
                out[ci, p] = {
                    "ar": ar, "v2": v2, "a_r": a_r,
                    "power": jnp.where(strict, g[:L2, :L2], 0.0),
                    "a_ak": jnp.where(strict, g[:L2, L2:], 0.0).astype(BF16),
                    "bk_tail": jnp.concatenate([stack(b_tail[rs]), stack(k_tail[rs])], axis=0),
                    "decay": g_incl[(i + 1) * L - 1:(i + 1) * L, sl]}
        return out

    def invert(fr, chunks):
        units = [(ci, p) for ci in chunks for p in pairs]
        power = {un: fr[un]["power"] for un in units}
        inv = {un: eye + power[un] for un in units}
        for _ in range(L.bit_length() - 2):
            power = {un: _dot(power[un], power[un]) for un in units}
            inv = {un: inv[un] + _dot(power[un], inv[un]) for un in units}
        for un in units:
            fr[un]["inv"] = inv[un]
            fr[un]["from_v"] = _dot(fr[un]["a_ak"], fr[un]["v2"])

    def advance(fr, chunks, states):
        ys = {p: [] for p in pairs}
        for ci in chunks:
            un = [fr[ci, p] for p in pairs]
            from_state = [_dot_nt(un[p]["ar"], states[p]) for p in pairs]
            uv = [jnp.concatenate([_dot(un[p]["inv"], from_state[p][:L2] + un[p]["from_v"]
                                        ).astype(BF16), un[p]["v2"]], axis=0) for p in pairs]
            for p in pairs:
                y2 = from_state[p][L2:] + _dot(un[p]["a_r"], uv[p])
                ys[p].append(y2[:L] + y2[L:])
            states = [states[p] * un[p]["decay"] + _dot_tn(uv[p], un[p]["bk_tail"]) for p in pairs]
        return ys, states

    def finish(fr, ys):
        lo = fr["lo"]
        for p in pairs:
            sl = sls[p]
            y = jnp.concatenate(ys[p], axis=0)
            mean = _pair_sum(y, lo) * (1.0 / HEAD_DIM)
            yc = y - mean
            var = _pair_sum(yc * yc, lo) * (1.0 / HEAD_DIM)
            yn = yc * lax.rsqrt(var + GN_EPS) * lng_ref[:, sl] + lnb_ref[:, sl]
            bonus = _pair_sum(fr["r"][:, sl] * fr["k"][:, sl] * rk_ref[:, sl], lo) * fr["v"][:, sl]
            o_ref[:, sl] = ((yn + bonus) * fr["gate"][:, sl]).astype(o_ref.dtype)

    chunks = list(range(n_chunks))
    last_row = u_ref[n_chunks * L - 1:n_chunks * L, :]
    states = [state_ref[p] for p in pairs]
    prepared = front(chunks)
    invert(prepared, chunks)
    ys, states = advance(prepared, chunks, states)
    finish(prepared, ys)
    prev_ref[...] = last_row
    for p in pairs:
        state_ref[p] = states[p]


def _rwkv(u_main, bsz, mu, w_up, a_up, g_up, w0, a0, k_k, k_a, r_k, lnx_g, lnx_b):
    n, width = u_main.shape
    step_rows = RWKV_CHUNK * RWKV_STEP_CHUNKS
    nc = n // bsz // step_rows
    rows = [w0, a0, k_k, k_a, r_k.reshape(-1), lnx_g, lnx_b]
    zeros_wa = jnp.zeros((D_DECAY_LORA, D_MAIN), F32)
    wup = jnp.concatenate([w_up, zeros_wa], axis=0)
    aup = jnp.concatenate([zeros_wa, a_up], axis=0)
    gup = jnp.concatenate([g_up, jnp.zeros((D_LORA_G - D_GATE_LORA, D_MAIN), F32)], axis=0)
    return pl.pallas_call(
        _rwkv_kernel,
        out_shape=jax.ShapeDtypeStruct((n, D_MAIN), BF16),
        grid=(bsz, nc),
        in_specs=[pl.BlockSpec((step_rows, width), lambda b, c: (b * nc + c, 0)),
                  _const_spec((1, width)),
                  _const_spec(wup.shape), _const_spec(aup.shape), _const_spec(gup.shape)]
                 + [_const_spec((1, D_MAIN))] * len(rows),
        out_specs=pl.BlockSpec((step_rows, D_MAIN), lambda b, c: (b * nc + c, 0)),
        scratch_shapes=[pltpu.VMEM((N_PAIRS, LANES, LANES), F32),
                        pltpu.VMEM((1, width), F32)],
        compiler_params=_params("parallel", "arbitrary"),
        name="rwkv7",
    )(u_main, mu[None, :], wup, aup, gup, *[x[None, :] for x in rows])


def _split3(x):
    hi = x.astype(BF16).astype(F32)
    mid = (x - hi).astype(BF16).astype(F32)
    lo = (x - hi - mid).astype(BF16).astype(F32)
    return hi, mid, lo


def _key_bias(c2, base):
    lane = lax.broadcasted_iota(jnp.int32, c2.shape, 1)
    terms = _split3(jnp.where(lane < N_MAIN_HEADS, -c2, 0.0))
    out = jnp.where(jnp.logical_and(lane >= base + 3 * N_MAIN_HEADS,
                                    lane < base + 3 * N_MAIN_HEADS + 3), 1.0, 0.0)
    for i, term in enumerate(terms):
        shift = base + i * N_MAIN_HEADS
        out = out + (pltpu.roll(term, shift, axis=1) if shift else term)
    return out


def _query_bias(lane, base, head, cq):
    out = jnp.zeros(lane.shape, F32)
    for i in range(3):
        out = jnp.where(lane == base + i * N_MAIN_HEADS + head, 1.0, out)
    for i, term in enumerate(_split3(cq)):
        out = jnp.where(lane == base + 3 * N_MAIN_HEADS + i, term, out)
    return out


def _shared_kv_kernel(x_ref, g_ref, w_ref, wvt_ref, bf_ref, kg_ref, k_ref, vt_ref, c_ref,
                      carry_ref):
    @pl.when(pl.program_id(1) == 0)
    def _():
        carry_ref[...] = jnp.zeros_like(carry_ref)

    tm = x_ref.shape[0]
    h = _rms(x_ref[...], g_ref[...]).astype(BF16)
    vt_ref[0] = _dot_nt(wvt_ref[...], h).astype(BF16)
    logits = jnp.dot(h, w_ref[:, D_MAIN:], preferred_element_type=F32) + bf_ref[...]
    log_f = -_softplus(-logits)
    tri = _tril(CUMSUM_BLOCK, strict=False).astype(F32)
    carry = carry_ref[...]
    for j in range(0, tm, CUMSUM_BLOCK):
        cj = _dot_f32(tri, log_f[j:j + CUMSUM_BLOCK]) + carry
        c_ref[j:j + CUMSUM_BLOCK, :] = cj * LOG2E
        carry = cj[CUMSUM_BLOCK - 1:CUMSUM_BLOCK, :]
    carry_ref[...] = carry

    c2 = c_ref[...]
    lo = _lo_mask((tm, LANES))
    bias = (_key_bias(c2, HEAD_DIM), _key_bias(c2, 0))
    for p in range(N_PAIRS):
        sl = slice(p * LANES, (p + 1) * LANES)
        kn = _head_rms(jnp.dot(h, w_ref[:, sl], preferred_element_type=F32), kg_ref[:, sl])
        for j in range(2):
            head = 2 * p + j
            own = lo if j == 0 else jnp.logical_not(lo)
            k_ref[:, head * LANES:(head + 1) * LANES] = jnp.where(own, kn, bias[j]).astype(BF16)


def _shared_kv(x, bsz, kv_norm, w_kv, b_f, k_gain):
    n, d = x.shape
    tm = ROW_BLOCK
    nt = n // bsz // tm
    pad = LANES - N_MAIN_HEADS
    w = jnp.concatenate([w_kv[:, :D_MAIN], jnp.pad(w_kv[:, 2 * D_MAIN:], ((0, 0), (0, pad)))],
                        axis=1).astype(BF16)
    w_vt = w_kv[:, D_MAIN:2 * D_MAIN].T.astype(BF16)
    row = lambda b, i: (b * nt + i, 0)
    return pl.pallas_call(
        _shared_kv_kernel,
        out_shape=[jax.ShapeDtypeStruct((n, N_MAIN_HEADS * LANES), BF16),
                   jax.ShapeDtypeStruct((bsz, D_MAIN, n // bsz), BF16),
                   jax.ShapeDtypeStruct((n, LANES), F32)],
        grid=(bsz, nt),
        in_specs=[pl.BlockSpec((tm, d), row), _const_spec((1, d)), _const_spec(w.shape),
                  _const_spec(w_vt.shape), _const_spec((1, LANES)), _const_spec((1, D_MAIN))],
        out_specs=[pl.BlockSpec((tm, N_MAIN_HEADS * LANES), row),
                   pl.BlockSpec((1, D_MAIN, tm), lambda b, i: (b, 0, i)),
                   pl.BlockSpec((tm, LANES), row)],
        scratch_shapes=[pltpu.VMEM((1, LANES), F32)],
        compiler_params=_params("parallel", "arbitrary"),
        name="shared_kv",
    )(x, kv_norm[None, :], w, w_vt, jnp.pad(b_f, (0, pad))[None, :],
      jnp.tile(k_gain, N_MAIN_HEADS)[None, :])


def _fox_kernel(q_ref, k0_ref, k1_ref, vt_ref, c_ref, cend_ref, gate_ref, qg_ref, thr_ref, o_ref,
                sa_ref, sb_ref, p_ref, acc_ref):
    tq = q_ref.shape[1]
    p = pl.program_id(1)
    qi = pl.program_id(2)
    c_ends = cend_ref[0]
    head_lane = lax.broadcasted_iota(jnp.int32, c_ends.shape, 1) >> 1
    keep = jnp.where(jnp.logical_and(head_lane == p, c_ref[0, 0:1, :] - c_ends >= thr_ref[...]),
                     1.0, 0.0)
    skippable = jnp.sum(keep, axis=1, keepdims=True) == 0.0
    first = jnp.sum(jnp.where(skippable, 1, 0)).astype(jnp.int32)
    qn = _head_rms(q_ref[0], qg_ref[...]) * (ATTN_SCALE * LOG2E)
    lane = lax.broadcasted_iota(jnp.int32, qn.shape, 1)
    lo = lane < HEAD_DIM
    c2 = c_ref[0]
    k_refs = (k0_ref, k1_ref)
    q_tiles = []
    for j in range(2):
        own = lo if j == 0 else jnp.logical_not(lo)
        cq = jnp.sum(jnp.where(lane == 2 * p + j, c2, 0.0), axis=-1, keepdims=True)
        bias = _query_bias(lane, HEAD_DIM * (1 - j), 2 * p + j, cq)
        q_tiles.append(jnp.where(own, qn, bias).astype(BF16))
    acc_ref[...] = jnp.zeros_like(acc_ref)
    tk = sa_ref.shape[1]
    key_idx = lax.broadcasted_iota(jnp.int32, (tk, LANES), 0)
    qry_idx = lax.broadcasted_iota(jnp.int32, (tk, LANES), 1)

    def scores(block, s_ref, q_lo=0):
        start = pl.multiple_of(block * tk, tk)
        for j in range(2):
            s_ref[j, :, q_lo:] = _dot_nt(k_refs[j][0, pl.ds(start, tk), :],
                                         q_tiles[j][q_lo:, :])

    def step(block, s_ref, slot, carry, key_offset, q_lo=0):
        keys = pl.ds(pl.multiple_of(block * tk, tk), tk)
        ones = jnp.ones((SUM_ROWS, tk), BF16)
        out = []
        for j in range(2):
            vtb = jnp.concatenate([vt_ref[0, j * HEAD_DIM:(j + 1) * HEAD_DIM, keys], ones], axis=0)
            m_old = carry[j]
            m_parts, a_parts = [m_old[:, :q_lo]] if q_lo else [], []
            for g in range(q_lo, tq, LANES):
                gs = slice(g, g + LANES)
                s = s_ref[j, :, gs]
                if key_offset is not None:
                    s = jnp.where(key_idx + key_offset <= qry_idx + g, s, NEG_INF)
                m_new = jnp.maximum(m_old[:, gs], jnp.max(s, axis=0, keepdims=True))
                m_parts.append(m_new)
                a_parts.append(jnp.exp2(m_old[:, gs] - m_new))
                p_ref[slot, j, :, gs] = jnp.exp2((s - m_new).astype(BF16))
            alpha = jnp.concatenate(a_parts, axis=1)
            acc_ref[j, :, q_lo:] = acc_ref[j, :, q_lo:] * alpha + jnp.dot(
                vtb, p_ref[slot, j, :, q_lo:], preferred_element_type=F32)
            out.append(jnp.concatenate(m_parts, axis=1))
        return tuple(out)

    def body(t, carry):
        scores(2 * t + 1, sb_ref)
        carry = step(2 * t, sa_ref, 0, carry, None)
        scores(2 * t + 2, sa_ref)
        return step(2 * t + 1, sb_ref, 1, carry, None)

    init = jnp.full((1, tq), NEG_INF, F32)
    scores(2 * first, sa_ref)
    carry = lax.fori_loop(first, qi, body, (init, init))
    scores(2 * qi + 1, sb_ref, tk)
    carry = step(2 * qi, sa_ref, 0, carry, 0)
    step(2 * qi + 1, sb_ref, 1, carry, tk, tk)
    o = jnp.concatenate([acc_ref[j, :HEAD_DIM, :] / acc_ref[j, HEAD_DIM:HEAD_DIM + 1, :]
                         for j in range(2)], axis=0).T
    o_ref[0] = (o * _sigmoid(gate_ref[0])).astype(o_ref.dtype)


def _fox(q, gate, k_tiles, v_t, c2, q_gain, k_gain, bsz):
    n, _ = q.shape
    t = n // bsz
    tq = ATTN_BLOCK
    c2 = c2.reshape(bsz, t, LANES)
    c_ends = c2.reshape(bsz, t // tq, tq, LANES)[:, :, tq - 1, :]
    qk_bound = ((1.01 * HEAD_DIM * ATTN_SCALE * LOG2E)
                * jnp.max(jnp.abs(q_gain)) * jnp.max(jnp.abs(k_gain)))
    thr = jnp.full((1, LANES), -(2.0 * qk_bound + UNDERFLOW_LOG2), F32)
    blk = pl.BlockSpec((1, tq, LANES), lambda b, p, i: (b, i, p))
    out = pl.pallas_call(
        _fox_kernel,
        out_shape=jax.ShapeDtypeStruct((bsz, t, D_MAIN), BF16),
        grid=(bsz, N_PAIRS, t // tq),
        in_specs=[blk,
                  pl.BlockSpec((1, t, LANES), lambda b, p, i: (b, 0, 2 * p)),
                  pl.BlockSpec((1, t, LANES), lambda b, p, i: (b, 0, 2 * p + 1)),
                  pl.BlockSpec((1, LANES, t), lambda b, p, i: (b, p, 0)),
                  pl.BlockSpec((1, tq, LANES), lambda b, p, i: (b, i, 0)),
                  pl.BlockSpec((1, t // tq, LANES), lambda b, p, i: (b, 0, 0)),
                  blk, _const_spec((1, LANES)), _const_spec((1, LANES))],
        out_specs=blk,
        scratch_shapes=[pltpu.VMEM((2, tq // 2, tq), F32), pltpu.VMEM((2, tq // 2, tq), F32),
                        pltpu.VMEM((2, 2, tq // 2, tq), BF16),
                        pltpu.VMEM((2, HEAD_DIM + SUM_ROWS, tq), F32)],
        compiler_params=_params("parallel", "parallel", "arbitrary"),
        name="fox_attn",
    )(q.reshape(bsz, t, D_MAIN), k_tiles.reshape(bsz, t, N_MAIN_HEADS * LANES),
      k_tiles.reshape(bsz, t, N_MAIN_HEADS * LANES), v_t,
      c2, c_ends, gate.reshape(bsz, t, D_MAIN), jnp.tile(q_gain, 2)[None, :], thr)
    return out.reshape(n, D_MAIN)


def _merge_ffn_kernel(x_ref, main_ref, mq_ref, mk_ref, mv_ref, mqg_ref, wo_ref, g_ref, wgu_ref,
                      wd_ref, o_ref):
    d_ff = wd_ref.shape[0]
    memo = _mem_attention(mq_ref[...], mk_ref[0, 0], mv_ref[0, 0], mqg_ref[...]).astype(BF16)
    mixed = (jnp.dot(main_ref[...], wo_ref[:D_MAIN, :], preferred_element_type=F32)
             + jnp.dot(memo, wo_ref[D_MAIN:, :], preferred_element_type=F32))
    x = x_ref[...] + mixed
    h = _rms(x, g_ref[...]).astype(BF16)
    acc = x
    for j in range(0, d_ff, FF_CHUNK):
        gj = jnp.dot(h, wgu_ref[:, j:j + FF_CHUNK], preferred_element_type=F32)
        uj = jnp.dot(h, wgu_ref[:, d_ff + j:d_ff + j + FF_CHUNK], preferred_element_type=F32)
        act = (gj * _sigmoid(gj) * uj).astype(BF16)
        acc = acc + jnp.dot(act, wd_ref[j:j + FF_CHUNK, :], preferred_element_type=F32)
    o_ref[...] = acc


def _merge_ffn(x, main, mem_q, k_mem, v_mem, mem_q_gain, layer, bsz, w_out, ffn_norm, w_gate_up,
               w_down):
    n, d = x.shape
    tm = ROW_BLOCK
    blocks_per_batch = n // bsz // tm
    row = lambda i: (i, 0)
    single = pl.Buffered(1)
    mem_spec = pl.BlockSpec((1, 1) + k_mem.shape[2:], lambda i: (layer, i // blocks_per_batch, 0, 0))
    return pl.pallas_call(
        _merge_ffn_kernel,
        out_shape=jax.ShapeDtypeStruct((n, d), F32),
        grid=(n // tm,),
        in_specs=[pl.BlockSpec((tm, d), row), pl.BlockSpec((tm, D_MAIN), row),
                  pl.BlockSpec((tm, D_MEMH), row), mem_spec, mem_spec, _const_spec((1, D_MEMH)),
                  pl.BlockSpec(w_out.shape, lambda i: (0, 0), pipeline_mode=single),
                  _const_spec((1, d)),
                  pl.BlockSpec(w_gate_up.shape, lambda i: (0, 0), pipeline_mode=single),
                  pl.BlockSpec(w_down.shape, lambda i: (0, 0), pipeline_mode=single)],
        out_specs=pl.BlockSpec((tm, d), row),
        compiler_params=_params("parallel"),
        name="merge_ffn",
    )(x, main, mem_q, k_mem, v_mem, jnp.tile(mem_q_gain, N_MEM_HEADS)[None, :],
      w_out.astype(BF16), ffn_norm[None, :], w_gate_up.astype(BF16), w_down.astype(BF16))


def _a_in_weights(w_in, mu):
    d_shift = 3 * D_MAIN + D_DECAY_LORA + D_AAA_LORA + D_GATE_LORA
    pad = D_LORA_G - D_GATE_LORA
    w = jnp.concatenate([w_in[:, :d_shift], jnp.zeros((w_in.shape[0], pad), w_in.dtype),
                         w_in[:, d_shift:]], axis=1)
    return w.astype(BF16), jnp.pad(mu, (0, pad))


def kernel(x, mem, mix_norm, w_out, mem_norm, w_mem_kv, mem_q_gain, mem_k_gain, ffn_norm, w_gate_up, w_down, a_w_in, a_mu, a_w0, a_w_up, a_a0, a_a_up, a_g_up, a_k_k, a_k_a, a_r_k, a_lnx_g, a_lnx_b, kv_norm, w_kv, b_f, k_gain, b_w_in, b_q_gain):
    bsz, t, d = x.shape
    n_a = a_w_in.shape[0]
    n_b = b_w_in.shape[0]
    x = x.reshape(bsz * t, d)
    k_mem, v_mem = _mem_kv(mem, mem_norm, w_mem_kv, mem_k_gain)

    def merge(x, l, main, mem_q):
        return _merge_ffn(x, main, mem_q, k_mem, v_mem, mem_q_gain[l], l, bsz, w_out[l],
                          ffn_norm[l], w_gate_up[l], w_down[l])

    for i in range(n_a):
        w, mu = _a_in_weights(a_w_in[i], a_mu[i])
        u_main, mem_q = _norm_proj(x, mix_norm[i][None, :], w, (D_A_MAIN, D_MEMH), (F32, F32))
        main = _rwkv(u_main, bsz, mu, a_w_up[i], a_a_up[i], a_g_up[i], a_w0[i], a_a0[i],
                     a_k_k[i], a_k_a[i], a_r_k[i], a_lnx_g[i], a_lnx_b[i])
        x = merge(x, i, main, mem_q)

    k_sh, v_sh, c_sh = _shared_kv(x, bsz, kv_norm, w_kv, b_f, k_gain)

    for j in range(n_b):
        l = n_a + j
        q, gate, mem_q = _norm_proj(x, mix_norm[l][None, :], b_w_in[j].astype(BF16),
                                    (D_MAIN, D_MAIN, D_MEMH), (F32, F32, F32))
        main = _fox(q, gate, k_sh, v_sh, c_sh, b_q_gain[j], k_gain, bsz)
        x = merge(x, l, main, mem_q)

    return x.reshape(bsz, t, d)
```

```python
import functools

import jax
import jax.numpy as jnp
from jax import lax
from jax.experimental import pallas as pl
from jax.experimental.pallas import tpu as pltpu

F32 = jnp.float32
BF16 = jnp.bfloat16
HIGHEST = lax.Precision.HIGHEST

HEAD_DIM = 64
LANES = 128
N_MAIN_HEADS = 12
N_PAIRS = N_MAIN_HEADS // 2
D_MAIN = N_MAIN_HEADS * HEAD_DIM
N_MEM_HEADS = 4
D_MEMH = N_MEM_HEADS * HEAD_DIM
D_DECAY_LORA = 64
D_AAA_LORA = 64
D_GATE_LORA = 160
D_LORA_WA = LANES
D_LORA_G = 2 * LANES
D_A_MAIN = 3 * D_MAIN + D_LORA_WA + D_LORA_G
RMS_EPS = 1e-6
GN_EPS = 64e-5
KK_NORM_FLOOR = 1e-24
ATTN_SCALE = HEAD_DIM ** -0.5
NEG_INF = -1e30
LOG2E = 1.4426950408889634
VMEM_LIMIT = 56 * 1024 * 1024

RWKV_CHUNK = 64
RWKV_STEP_CHUNKS = 4
ROW_BLOCK = 512
ATTN_BLOCK = 512
CUMSUM_BLOCK = 128
SUM_ROWS = 16
UNDERFLOW_LOG2 = 160.0


def _params(*sem):
    return pltpu.CompilerParams(dimension_semantics=sem, vmem_limit_bytes=VMEM_LIMIT)


def _const_spec(shape):
    nd = len(shape)
    return pl.BlockSpec(shape, lambda *_: (0,) * nd)


def _dot(a, b):
    return jnp.dot(a.astype(BF16), b.astype(BF16), preferred_element_type=F32)


def _dot_nt(a, b):
    return lax.dot_general(a.astype(BF16), b.astype(BF16), (((1,), (1,)), ((), ())),
                           preferred_element_type=F32)


def _dot_tn(a, b):
    return lax.dot_general(a.astype(BF16), b.astype(BF16), (((0,), (0,)), ((), ())),
                           preferred_element_type=F32)


def _dot_f32(a, b):
    return jnp.dot(a, b, preferred_element_type=F32, precision=HIGHEST)


def _sigmoid(x):
    return 1.0 / (1.0 + jnp.exp(-x))


def _softplus(x):
    return jnp.maximum(x, 0.0) + jnp.log(1.0 + jnp.exp(-jnp.abs(x)))


def _rms(x, g):
    return x * lax.rsqrt(jnp.mean(x * x, axis=-1, keepdims=True) + RMS_EPS) * g


def _lo_mask(shape):
    return lax.broadcasted_iota(jnp.int32, shape, len(shape) - 1) < HEAD_DIM


def _pair_sum(x, lo):
    s_lo = jnp.sum(jnp.where(lo, x, 0.0), axis=-1, keepdims=True)
    s_hi = jnp.sum(jnp.where(lo, 0.0, x), axis=-1, keepdims=True)
    return jnp.where(lo, s_lo, s_hi)


def _head_rms(x, gain):
    outs = []
    for j in range(x.shape[-1] // LANES):
        xb = x[:, j * LANES:(j + 1) * LANES]
        lo = _lo_mask(xb.shape)
        ms = _pair_sum(xb * xb, lo) * (1.0 / HEAD_DIM)
        outs.append(xb * lax.rsqrt(ms + RMS_EPS))
    y = outs[0] if len(outs) == 1 else jnp.concatenate(outs, axis=-1)
    return y * gain


def _tril(n, strict):
    r = lax.broadcasted_iota(jnp.int32, (n, n), 0)
    c = lax.broadcasted_iota(jnp.int32, (n, n), 1)
    return (r > c) if strict else (r >= c)


def _norm_proj_kernel(x_ref, g_ref, w_ref, *o_refs, splits):
    h = _rms(x_ref[...], g_ref[...]).astype(BF16)
    c0 = 0
    for o_ref, width in zip(o_refs, splits):
        o_ref[...] = jnp.dot(h, w_ref[:, c0:c0 + width],
                             preferred_element_type=F32).astype(o_ref.dtype)
        c0 += width


def _norm_proj(x, g, w, splits, dtypes):
    n, d = x.shape
    tm = ROW_BLOCK
    return pl.pallas_call(
        functools.partial(_norm_proj_kernel, splits=splits),
        out_shape=[jax.ShapeDtypeStruct((n, s), dt) for s, dt in zip(splits, dtypes)],
        grid=(n // tm,),
        in_specs=[pl.BlockSpec((tm, d), lambda i: (i, 0)),
                  _const_spec((1, d)),
                  _const_spec(w.shape)],
        out_specs=[pl.BlockSpec((tm, s), lambda i: (i, 0)) for s in splits],
        compiler_params=_params("parallel"),
        name="norm_proj",
    )(x, g, w)


def _mem_kv_kernel(mem_ref, g_ref, w_ref, kg_ref, k_ref, v_ref):
    h = _rms(mem_ref[0], g_ref[0]).astype(BF16)
    kv = jnp.dot(h, w_ref[0], preferred_element_type=F32)
    k_ref[0, 0] = _head_rms(kv[:, :D_MEMH], kg_ref[0]).astype(BF16)
    v_ref[0, 0] = kv[:, D_MEMH:].astype(BF16)


def _mem_kv(mem, mem_norm, w_mem_kv, mem_k_gain):
    b, m, d = mem.shape
    depth = w_mem_kv.shape[0]
    out = jax.ShapeDtypeStruct((depth, b, m, D_MEMH), BF16)
    return pl.pallas_call(
        _mem_kv_kernel,
        out_shape=[out, out],
        grid=(depth, b),
        in_specs=[pl.BlockSpec((1, m, d), lambda l, i: (i, 0, 0)),
                  pl.BlockSpec((1, 1, d), lambda l, i: (l, 0, 0)),
                  pl.BlockSpec((1, d, 2 * D_MEMH), lambda l, i: (l, 0, 0)),
                  pl.BlockSpec((1, 1, D_MEMH), lambda l, i: (l, 0, 0))],
        out_specs=[pl.BlockSpec((1, 1, m, D_MEMH), lambda l, i: (l, i, 0, 0))] * 2,
        compiler_params=_params("parallel", "parallel"),
        name="mem_kv",
    )(mem, mem_norm[:, None, :], w_mem_kv.astype(BF16),
      jnp.tile(mem_k_gain, (1, N_MEM_HEADS))[:, None, :])


def _mem_attention(q, k, v, q_gain):
    qn = _head_rms(q, q_gain) * ATTN_SCALE
    lane = lax.broadcasted_iota(jnp.int32, qn.shape, 1)
    out = jnp.zeros(qn.shape, F32)
    for h in range(N_MEM_HEADS):
        sel = jnp.logical_and(lane >= h * HEAD_DIM, lane < (h + 1) * HEAD_DIM)
        s = _dot_nt(jnp.where(sel, qn, 0.0), k)
        e = jnp.exp(s - jnp.max(s, axis=-1, keepdims=True))
        o = _dot(e, v) / jnp.sum(e, axis=-1, keepdims=True)
        out = jnp.where(sel, o, out)
    return out


def _rwkv_kernel(u_ref, mu_ref, wup_ref, aup_ref, gup_ref, w0_ref, a0_ref, kk_ref, ka_ref,
                 rk_ref, lng_ref, lnb_ref, o_ref, state_ref, prev_ref):
    L = RWKV_CHUNK
    L2 = 2 * L
    assert L2 == LANES
    n_chunks = u_ref.shape[0] // L
    c = pl.program_id(1)

    @pl.when(c == 0)
    def _():
        state_ref[...] = jnp.zeros_like(state_ref)
        prev_ref[...] = jnp.zeros_like(prev_ref)

    own = (lax.broadcasted_iota(jnp.int32, (L2, LANES), 0) < L) == _lo_mask((L2, LANES))
    tok_r = lax.broadcasted_iota(jnp.int32, (L2, L2), 0) & (L - 1)
    tok_c = lax.broadcasted_iota(jnp.int32, (L2, L2), 1) & (L - 1)
    strict = tok_r > tok_c
    incl = tok_r >= tok_c
    eye = (lax.broadcasted_iota(jnp.int32, (L2, L2), 0)
           == lax.broadcasted_iota(jnp.int32, (L2, L2), 1)).astype(F32)
    pairs = range(N_PAIRS)
    sls = [slice(p * LANES, (p + 1) * LANES) for p in pairs]

    def stack(x):
        return jnp.where(own, jnp.concatenate([x, x], axis=0), 0.0)

    def front(chunks):
        rows = len(chunks) * L
        u = u_ref[...]
        row = lax.broadcasted_iota(jnp.int32, u.shape, 0)
        us = u + (jnp.where(row == 0, prev_ref[...], pltpu.roll(u, 1, axis=0)) - u) * mu_ref[...]

        r = us[:, 0:D_MAIN]
        k = us[:, D_MAIN:2 * D_MAIN]
        v = us[:, 2 * D_MAIN:3 * D_MAIN]
        x_wa = us[:, 3 * D_MAIN:3 * D_MAIN + D_LORA_WA]
        x_g = us[:, 3 * D_MAIN + D_LORA_WA:]

        w_log = -_softplus(-(w0_ref[...] + _dot(jnp.tanh(x_wa), wup_ref[...]))) - 0.5
        log_w = jnp.exp(w_log) * (-LOG2E)
        lr = _sigmoid(a0_ref[...] + _dot(x_wa, aup_ref[...]))
        gate = _dot(_sigmoid(x_g), gup_ref[...])
        kk_raw = k * kk_ref[...]
        k = k * (1.0 + (lr - 1.0) * ka_ref[...])

        tok_i = lax.broadcasted_iota(jnp.int32, (rows, rows), 0)
        tok_j = lax.broadcasted_iota(jnp.int32, (rows, rows), 1)
        shift = L.bit_length() - 1
        same_chunk_tril = jnp.logical_and(tok_i >= tok_j, (tok_i >> shift) == (tok_j >> shift))
        cum = _dot_f32(same_chunk_tril.astype(F32), log_w)
        g_incl = jnp.exp2(cum)
        g_excl = jnp.exp2(cum - log_w)
        g_inv = jnp.exp2(-cum)
        g_end = jnp.concatenate(
            [jnp.broadcast_to(g_incl[(i + 1) * L - 1:(i + 1) * L, :], (L, D_MAIN))
             for i in range(len(chunks))], axis=0)
        g_tail = g_end * g_inv

        lo = _lo_mask((rows, LANES))
        out = {"r": r, "k": k, "v": v, "gate": gate, "lo": lo}
        for p in pairs:
            sl = sls[p]
            kk = kk_raw[:, sl]
            kkn = kk * lax.rsqrt(jnp.maximum(_pair_sum(kk * kk, lo), KK_NORM_FLOOR))
            b = kkn * lr[:, sl]
            a_dec = -kkn * g_excl[:, sl]
            r_dec = r[:, sl] * g_incl[:, sl]
            b_inv = b * g_inv[:, sl]
            k_inv = k[:, sl] * g_inv[:, sl]
            b_tail = b * g_tail[:, sl]
            k_tail = k[:, sl] * g_tail[:, sl]
            for i, ci in enumerate(chunks):
                rs = slice(i * L, (i + 1) * L)
                ar = jnp.concatenate([stack(a_dec[rs]), stack(r_dec[rs])],
                                     axis=0).astype(BF16)
                v2 = stack(v[rs, sl]).astype(BF16)
                bk_inv = jnp.concatenate([stack(b_inv[rs]), stack(k_inv[rs])], axis=0)
                g = _dot_nt(ar, bk_inv)
                a_r = jnp.concatenate([jnp.where(incl, g[L2:, :L2], 0.0),
                                       jnp.where(incl, g[L2:, L2:], 0.0)],
                                      axis=1).astype(BF16)
                out[ci, p] = {
                    "ar": ar, "v2": v2, "a_r": a_r,
                    "power": jnp.where(strict, g[:L2, :L2], 0.0),
                    "a_ak": jnp.where(strict, g[:L2, L2:], 0.0).astype(BF16),
                    "bk_tail": jnp.concatenate([stack(b_tail[rs]), stack(k_tail[rs])], axis=0),
                    "decay": g_incl[(i + 1) * L - 1:(i + 1) * L, sl]}
        return out

    def invert(fr, chunks):
        units = [(ci, p) for ci in chunks for p in pairs]
        power = {un: fr[un]["power"] for un in units}
        inv = {un: eye + power[un] for un in units}
        for _ in range(L.bit_length() - 2):
            power = {un: _dot(power[un], power[un]) for un in units}
            inv = {un: inv[un] + _dot(power[un], inv[un]) for un in units}
        for un in units:
            fr[un]["inv"] = inv[un]
            fr[un]["from_v"] = _dot(fr[un]["a_ak"], fr[un]["v2"])

    def advance(fr, chunks, states):
        ys = {p: [] for p in pairs}
        for ci in chunks:
            un = [fr[ci, p] for p in pairs]
            from_state = [_dot_nt(un[p]["ar"], states[p]) for p in pairs]
            uv = [jnp.concatenate([_dot(un[p]["inv"], from_state[p][:L2] + un[p]["from_v"]
                                        ).astype(BF16), un[p]["v2"]], axis=0) for p in pairs]
            for p in pairs:
                y2 = from_state[p][L2:] + _dot(un[p]["a_r"], uv[p])
                ys[p].append(y2[:L] + y2[L:])
            states = [states[p] * un[p]["decay"] + _dot_tn(uv[p], un[p]["bk_tail"]) for p in pairs]
        return ys, states

    def finish(fr, ys):
        lo = fr["lo"]
        for p in pairs:
            sl = sls[p]
            y = jnp.concatenate(ys[p], axis=0)
            mean = _pair_sum(y, lo) * (1.0 / HEAD_DIM)
            yc = y - mean
            var = _pair_sum(yc * yc, lo) * (1.0 / HEAD_DIM)
            yn = yc * lax.rsqrt(var + GN_EPS) * lng_ref[:, sl] + lnb_ref[:, sl]
            bonus = _pair_sum(fr["r"][:, sl] * fr["k"][:, sl] * rk_ref[:, sl], lo) * fr["v"][:, sl]
            o_ref[:, sl] = ((yn + bonus) * fr["gate"][:, sl]).astype(o_ref.dtype)

    chunks = list(range(n_chunks))
    last_row = u_ref[n_chunks * L - 1:n_chunks * L, :]
    states = [state_ref[p] for p in pairs]
    prepared = front(chunks)
    invert(prepared, chunks)
    ys, states = advance(prepared, chunks, states)
    finish(prepared, ys)
    prev_ref[...] = last_row
    for p in pairs:
        state_ref[p] = states[p]


def _rwkv(u_main, bsz, mu, w_up, a_up, g_up, w0, a0, k_k, k_a, r_k, lnx_g, lnx_b):
    n, width = u_main.shape
    step_rows = RWKV_CHUNK * RWKV_STEP_CHUNKS
    nc = n // bsz // step_rows
    rows = [w0, a0, k_k, k_a, r_k.reshape(-1), lnx_g, lnx_b]
    zeros_wa = jnp.zeros((D_DECAY_LORA, D_MAIN), F32)
    wup = jnp.concatenate([w_up, zeros_wa], axis=0)
    aup = jnp.concatenate([zeros_wa, a_up], axis=0)
    gup = jnp.concatenate([g_up, jnp.zeros((D_LORA_G - D_GATE_LORA, D_MAIN), F32)], axis=0)
    return pl.pallas_call(
        _rwkv_kernel,
        out_shape=jax.ShapeDtypeStruct((n, D_MAIN), BF16),
        grid=(bsz, nc),
        in_specs=[pl.BlockSpec((step_rows, width), lambda b, c: (b * nc + c, 0)),
                  _const_spec((1, width)),
                  _const_spec(wup.shape), _const_spec(aup.shape), _const_spec(gup.shape)]
                 + [_const_spec((1, D_MAIN))] * len(rows),
        out_specs=pl.BlockSpec((step_rows, D_MAIN), lambda b, c: (b * nc + c, 0)),
        scratch_shapes=[pltpu.VMEM((N_PAIRS, LANES, LANES), F32),
                        pltpu.VMEM((1, width), F32)],
        compiler_params=_params("parallel", "arbitrary"),
        name="rwkv7",
    )(u_main, mu[None, :], wup, aup, gup, *[x[None, :] for x in rows])


def _split3(x):
    hi = x.astype(BF16).astype(F32)
    mid = (x - hi).astype(BF16).astype(F32)
    lo = (x - hi - mid).astype(BF16).astype(F32)
    return hi, mid, lo


def _key_bias(c2, base):
    lane = lax.broadcasted_iota(jnp.int32, c2.shape, 1)
    terms = _split3(jnp.where(lane < N_MAIN_HEADS, -c2, 0.0))
    out = jnp.where(jnp.logical_and(lane >= base + 3 * N_MAIN_HEADS,
                                    lane < base + 3 * N_MAIN_HEADS + 3), 1.0, 0.0)
    for i, term in enumerate(terms):
        shift = base + i * N_MAIN_HEADS
        out = out + (pltpu.roll(term, shift, axis=1) if shift else term)
    return out


def _query_bias(lane, base, head, cq):
    out = jnp.zeros(lane.shape, F32)
    for i in range(3):
        out = jnp.where(lane == base + i * N_MAIN_HEADS + head, 1.0, out)
    for i, term in enumerate(_split3(cq)):
        out = jnp.where(lane == base + 3 * N_MAIN_HEADS + i, term, out)
    return out


def _shared_kv_kernel(x_ref, g_ref, w_ref, wvt_ref, bf_ref, kg_ref, k_ref, vt_ref, c_ref,
                      carry_ref):
    @pl.when(pl.program_id(1) == 0)
    def _():
        carry_ref[...] = jnp.zeros_like(carry_ref)

    tm = x_ref.shape[0]
    h = _rms(x_ref[...], g_ref[...]).astype(BF16)
    vt_ref[0] = _dot_nt(wvt_ref[...], h).astype(BF16)
    logits = jnp.dot(h, w_ref[:, D_MAIN:], preferred_element_type=F32) + bf_ref[...]
    log_f = -_softplus(-logits)
    tri = _tril(CUMSUM_BLOCK, strict=False).astype(F32)
    carry = carry_ref[...]
    for j in range(0, tm, CUMSUM_BLOCK):
        cj = _dot_f32(tri, log_f[j:j + CUMSUM_BLOCK]) + carry
        c_ref[j:j + CUMSUM_BLOCK, :] = cj * LOG2E
        carry = cj[CUMSUM_BLOCK - 1:CUMSUM_BLOCK, :]
    carry_ref[...] = carry

    c2 = c_ref[...]
    lo = _lo_mask((tm, LANES))
    bias = (_key_bias(c2, HEAD_DIM), _key_bias(c2, 0))
    for p in range(N_PAIRS):
        sl = slice(p * LANES, (p + 1) * LANES)
        kn = _head_rms(jnp.dot(h, w_ref[:, sl], preferred_element_type=F32), kg_ref[:, sl])
        for j in range(2):
            head = 2 * p + j
            own = lo if j == 0 else jnp.logical_not(lo)
            k_ref[:, head * LANES:(head + 1) * LANES] = jnp.where(own, kn, bias[j]).astype(BF16)


def _shared_kv(x, bsz, kv_norm, w_kv, b_f, k_gain):
    n, d = x.shape
    tm = ROW_BLOCK
    nt = n // bsz // tm
    pad = LANES - N_MAIN_HEADS
    w = jnp.concatenate([w_kv[:, :D_MAIN], jnp.pad(w_kv[:, 2 * D_MAIN:], ((0, 0), (0, pad)))],
                        axis=1).astype(BF16)
    w_vt = w_kv[:, D_MAIN:2 * D_MAIN].T.astype(BF16)
    row = lambda b, i: (b * nt + i, 0)
    return pl.pallas_call(
        _shared_kv_kernel,
        out_shape=[jax.ShapeDtypeStruct((n, N_MAIN_HEADS * LANES), BF16),
                   jax.ShapeDtypeStruct((bsz, D_MAIN, n // bsz), BF16),
                   jax.ShapeDtypeStruct((n, LANES), F32)],
        grid=(bsz, nt),
        in_specs=[pl.BlockSpec((tm, d), row), _const_spec((1, d)), _const_spec(w.shape),
                  _const_spec(w_vt.shape), _const_spec((1, LANES)), _const_spec((1, D_MAIN))],
        out_specs=[pl.BlockSpec((tm, N_MAIN_HEADS * LANES), row),
                   pl.BlockSpec((1, D_MAIN, tm), lambda b, i: (b, 0, i)),
                   pl.BlockSpec((tm, LANES), row)],
        scratch_shapes=[pltpu.VMEM((1, LANES), F32)],
        compiler_params=_params("parallel", "arbitrary"),
        name="shared_kv",
    )(x, kv_norm[None, :], w, w_vt, jnp.pad(b_f, (0, pad))[None, :],
      jnp.tile(k_gain, N_MAIN_HEADS)[None, :])


def _fox_kernel(q_ref, k0_ref, k1_ref, vt_ref, c_ref, cend_ref, gate_ref, qg_ref, thr_ref, o_ref,
                sa_ref, sb_ref, p_ref, acc_ref):
    tq = q_ref.shape[1]
    p = pl.program_id(1)
    qi = pl.program_id(2)
    c_ends = cend_ref[0]
    head_lane = lax.broadcasted_iota(jnp.int32, c_ends.shape, 1) >> 1
    keep = jnp.where(jnp.logical_and(head_lane == p, c_ref[0, 0:1, :] - c_ends >= thr_ref[...]),
                     1.0, 0.0)
    skippable = jnp.sum(keep, axis=1, keepdims=True) == 0.0
    first = jnp.sum(jnp.where(skippable, 1, 0)).astype(jnp.int32)
    qn = _head_rms(q_ref[0], qg_ref[...]) * (ATTN_SCALE * LOG2E)
    lane = lax.broadcasted_iota(jnp.int32, qn.shape, 1)
    lo = lane < HEAD_DIM
    c2 = c_ref[0]
    k_refs = (k0_ref, k1_ref)
    q_tiles = []
    for j in range(2):
        own = lo if j == 0 else jnp.logical_not(lo)
        cq = jnp.sum(jnp.where(lane == 2 * p + j, c2, 0.0), axis=-1, keepdims=True)
        bias = _query_bias(lane, HEAD_DIM * (1 - j), 2 * p + j, cq)
        q_tiles.append(jnp.where(own, qn, bias).astype(BF16))
    acc_ref[...] = jnp.zeros_like(acc_ref)
    tk = sa_ref.shape[1]
    key_idx = lax.broadcasted_iota(jnp.int32, (tk, LANES), 0)
    qry_idx = lax.broadcasted_iota(jnp.int32, (tk, LANES), 1)

    def scores(block, s_ref, q_lo=0):
        start = pl.multiple_of(block * tk, tk)
        for j in range(2):
            s_ref[j, :, q_lo:] = _dot_nt(k_refs[j][0, pl.ds(start, tk), :],
                                         q_tiles[j][q_lo:, :])

    def step(block, s_ref, slot, carry, key_offset, q_lo=0):
        keys = pl.ds(pl.multiple_of(block * tk, tk), tk)
        ones = jnp.ones((SUM_ROWS, tk), BF16)
        out = []
        for j in range(2):
            vtb = jnp.concatenate([vt_ref[0, j * HEAD_DIM:(j + 1) * HEAD_DIM, keys], ones], axis=0)
            m_old = carry[j]
            m_parts, a_parts = [m_old[:, :q_lo]] if q_lo else [], []
            for g in range(q_lo, tq, LANES):
                gs = slice(g, g + LANES)
                s = s_ref[j, :, gs]
                if key_offset is not None:
                    s = jnp.where(key_idx + key_offset <= qry_idx + g, s, NEG_INF)
                m_new = jnp.maximum(m_old[:, gs], jnp.max(s, axis=0, keepdims=True))
                m_parts.append(m_new)
                a_parts.append(jnp.exp2(m_old[:, gs] - m_new))
                p_ref[slot, j, :, gs] = jnp.exp2((s - m_new).astype(BF16))
            alpha = jnp.concatenate(a_parts, axis=1)
            acc_ref[j, :, q_lo:] = acc_ref[j, :, q_lo:] * alpha + jnp.dot(
                vtb, p_ref[slot, j, :, q_lo:], preferred_element_type=F32)
            out.append(jnp.concatenate(m_parts, axis=1))
        return tuple(out)

    def body(t, carry):
        scores(2 * t + 1, sb_ref)
        carry = step(2 * t, sa_ref, 0, carry, None)
        scores(2 * t + 2, sa_ref)
        return step(2 * t + 1, sb_ref, 1, carry, None)

    init = jnp.full((1, tq), NEG_INF, F32)
    scores(2 * first, sa_ref)
    carry = lax.fori_loop(first, qi, body, (init, init))
    scores(2 * qi + 1, sb_ref, tk)
    carry = step(2 * qi, sa_ref, 0, carry, 0)
    step(2 * qi + 1, sb_ref, 1, carry, tk, tk)
    o = jnp.concatenate([acc_ref[j, :HEAD_DIM, :] / acc_ref[j, HEAD_DIM:HEAD_DIM + 1, :]
                         for j in range(2)], axis=0).T
    o_ref[0] = (o * _sigmoid(gate_ref[0])).astype(o_ref.dtype)


def _fox(q, gate, k_tiles, v_t, c2, q_gain, k_gain, bsz):
    n, _ = q.shape
    t = n // bsz
    tq = ATTN_BLOCK
    c2 = c2.reshape(bsz, t, LANES)
    c_ends = c2.reshape(bsz, t // tq, tq, LANES)[:, :, tq - 1, :]
    qk_bound = ((1.01 * HEAD_DIM * ATTN_SCALE * LOG2E)
                * jnp.max(jnp.abs(q_gain)) * jnp.max(jnp.abs(k_gain)))
    thr = jnp.full((1, LANES), -(2.0 * qk_bound + UNDERFLOW_LOG2), F32)
    blk = pl.BlockSpec((1, tq, LANES), lambda b, p, i: (b, i, p))
    out = pl.pallas_call(
        _fox_kernel,
        out_shape=jax.ShapeDtypeStruct((bsz, t, D_MAIN), BF16),
        grid=(bsz, N_PAIRS, t // tq),
        in_specs=[blk,
                  pl.BlockSpec((1, t, LANES), lambda b, p, i: (b, 0, 2 * p)),
                  pl.BlockSpec((1, t, LANES), lambda b, p, i: (b, 0, 2 * p + 1)),
                  pl.BlockSpec((1, LANES, t), lambda b, p, i: (b, p, 0)),
                  pl.BlockSpec((1, tq, LANES), lambda b, p, i: (b, i, 0)),
                  pl.BlockSpec((1, t // tq, LANES), lambda b, p, i: (b, 0, 0)),
                  blk, _const_spec((1, LANES)), _const_spec((1, LANES))],
        out_specs=blk,
        scratch_shapes=[pltpu.VMEM((2, tq // 2, tq), F32), pltpu.VMEM((2, tq // 2, tq), F32),
                        pltpu.VMEM((2, 2, tq // 2, tq), BF16),
                        pltpu.VMEM((2, HEAD_DIM + SUM_ROWS, tq), F32)],
        compiler_params=_params("parallel", "parallel", "arbitrary"),
        name="fox_attn",
    )(q.reshape(bsz, t, D_MAIN), k_tiles.reshape(bsz, t, N_MAIN_HEADS * LANES),
      k_tiles.reshape(bsz, t, N_MAIN_HEADS * LANES), v_t,
      c2, c_ends, gate.reshape(bsz, t, D_MAIN), jnp.tile(q_gain, 2)[None, :], thr)
    return out.reshape(n, D_MAIN)


def _merge_ffn_kernel(x_ref, main_ref, mq_ref, mk_ref, mv_ref, mqg_ref, wo_ref, g_ref, wgu_ref,
                      wd_ref, o_ref):
    d_ff = wd_ref.shape[0]
    memo = _mem_attention(mq_ref[...], mk_ref[0, 0], mv_ref[0, 0], mqg_ref[...]).astype(BF16)
    mixed = (jnp.dot(main_ref[...], wo_ref[:D_MAIN, :], preferred_element_type=F32)
             + jnp.dot(memo, wo_ref[D_MAIN:, :], preferred_element_type=F32))
    x = x_ref[...] + mixed
    h = _rms(x, g_ref[...]).astype(BF16)
    gate = jnp.dot(h, wgu_ref[:, :d_ff], preferred_element_type=F32)
    up = jnp.dot(h, wgu_ref[:, d_ff:], preferred_element_type=F32)
    act = (gate * _sigmoid(gate) * up).astype(BF16)
    o_ref[...] = x + jnp.dot(act, wd_ref[...], preferred_element_type=F32)


def _merge_ffn(x, main, mem_q, k_mem, v_mem, mem_q_gain, layer, bsz, w_out, ffn_norm, w_gate_up,
               w_down):
    n, d = x.shape
    tm = ROW_BLOCK
    blocks_per_batch = n // bsz // tm
    row = lambda i: (i, 0)
    single = pl.Buffered(1)
    mem_spec = pl.BlockSpec((1, 1) + k_mem.shape[2:], lambda i: (layer, i // blocks_per_batch, 0, 0))
    return pl.pallas_call(
        _merge_ffn_kernel,
        out_shape=jax.ShapeDtypeStruct((n, d), F32),
        grid=(n // tm,),
        in_specs=[pl.BlockSpec((tm, d), row), pl.BlockSpec((tm, D_MAIN), row),
                  pl.BlockSpec((tm, D_MEMH), row), mem_spec, mem_spec, _const_spec((1, D_MEMH)),
                  pl.BlockSpec(w_out.shape, lambda i: (0, 0), pipeline_mode=single),
                  _const_spec((1, d)),
                  pl.BlockSpec(w_gate_up.shape, lambda i: (0, 0), pipeline_mode=single),
                  pl.BlockSpec(w_down.shape, lambda i: (0, 0), pipeline_mode=single)],
        out_specs=pl.BlockSpec((tm, d), row),
        compiler_params=_params("parallel"),
        name="merge_ffn",
    )(x, main, mem_q, k_mem, v_mem, jnp.tile(mem_q_gain, N_MEM_HEADS)[None, :],
      w_out.astype(BF16), ffn_norm[None, :], w_gate_up.astype(BF16), w_down.astype(BF16))


def _a_in_weights(w_in, mu):
    d_shift = 3 * D_MAIN + D_DECAY_LORA + D_AAA_LORA + D_GATE_LORA
    pad = D_LORA_G - D_GATE_LORA
    w = jnp.concatenate([w_in[:, :d_shift], jnp.zeros((w_in.shape[0], pad), w_in.dtype),
                         w_in[:, d_shift:]], axis=1)
    return w.astype(BF16), jnp.pad(mu, (0, pad))


def kernel(x, mem, mix_norm, w_out, mem_norm, w_mem_kv, mem_q_gain, mem_k_gain, ffn_norm, w_gate_up, w_down, a_w_in, a_mu, a_w0, a_w_up, a_a0, a_a_up, a_g_up, a_k_k, a_k_a, a_r_k, a_lnx_g, a_lnx_b, kv_norm, w_kv, b_f, k_gain, b_w_in, b_q_gain):
    bsz, t, d = x.shape
    n_a = a_w_in.shape[0]
    n_b = b_w_in.shape[0]
    x = x.reshape(bsz * t, d)
    k_mem, v_mem = _mem_kv(mem, mem_norm, w_mem_kv, mem_k_gain)

    def merge(x, l, main, mem_q):
        return _merge_ffn(x, main, mem_q, k_mem, v_mem, mem_q_gain[l], l, bsz, w_out[l],
                          ffn_norm[l], w_gate_up[l], w_down[l])

    for i in range(n_a):
        w, mu = _a_in_weights(a_w_in[i], a_mu[i])
        u_main, mem_q = _norm_proj(x, mix_norm[i][None, :], w, (D_A_MAIN, D_MEMH), (F32, F32))
        main = _rwkv(u_main, bsz, mu, a_w_up[i], a_a_up[i], a_g_up[i], a_w0[i], a_a0[i],
                     a_k_k[i], a_k_a[i], a_r_k[i], a_lnx_g[i], a_lnx_b[i])
        x = merge(x, i, main, mem_q)

    k_sh, v_sh, c_sh = _shared_kv(x, bsz, kv_norm, w_kv, b_f, k_gain)

    for j in range(n_b):
        l = n_a + j
        q, gate, mem_q = _norm_proj(x, mix_norm[l][None, :], b_w_in[j].astype(BF16),
                                    (D_MAIN, D_MAIN, D_MEMH), (F32, F32, F32))
        main = _fox(q, gate, k_sh, v_sh, c_sh, b_q_gain[j], k_gain, bsz)
        x = merge(x, l, main, mem_q)

    return x.reshape(bsz, t, d)
```

```python
import functools

import jax
import jax.numpy as jnp
from jax import lax
from jax.experimental import pallas as pl
from jax.experimental.pallas import tpu as pltpu

F32 = jnp.float32
BF16 = jnp.bfloat16
HIGHEST = lax.Precision.HIGHEST

HEAD_DIM = 64
LANES = 128
N_MAIN_HEADS = 12
N_PAIRS = N_MAIN_HEADS // 2
D_MAIN = N_MAIN_HEADS * HEAD_DIM
N_MEM_HEADS = 4
D_MEMH = N_MEM_HEADS * HEAD_DIM
D_DECAY_LORA = 64
D_AAA_LORA = 64
D_GATE_LORA = 160
D_LORA_WA = LANES
D_LORA_G = 2 * LANES
D_A_MAIN = 3 * D_MAIN + D_LORA_WA + D_LORA_G
RMS_EPS = 1e-6
GN_EPS = 64e-5
KK_NORM_FLOOR = 1e-24
ATTN_SCALE = HEAD_DIM ** -0.5
NEG_INF = -1e30
LOG2E = 1.4426950408889634
VMEM_LIMIT = 56 * 1024 * 1024

RWKV_CHUNK = 64
RWKV_STEP_CHUNKS = 4
ROW_BLOCK = 512
ATTN_BLOCK = 512
CUMSUM_BLOCK = 128
SUM_ROWS = 16
UNDERFLOW_LOG2 = 160.0


def _params(*sem):
    return pltpu.CompilerParams(dimension_semantics=sem, vmem_limit_bytes=VMEM_LIMIT)


def _const_spec(shape):
    nd = len(shape)
    return pl.BlockSpec(shape, lambda *_: (0,) * nd)


def _dot(a, b):
    return jnp.dot(a.astype(BF16), b.astype(BF16), preferred_element_type=F32)


def _dot_nt(a, b):
    return lax.dot_general(a.astype(BF16), b.astype(BF16), (((1,), (1,)), ((), ())),
                           preferred_element_type=F32)


def _dot_tn(a, b):
    return lax.dot_general(a.astype(BF16), b.astype(BF16), (((0,), (0,)), ((), ())),
                           preferred_element_type=F32)


def _dot_f32(a, b):
    return jnp.dot(a, b, preferred_element_type=F32, precision=HIGHEST)


def _sigmoid(x):
    return 1.0 / (1.0 + jnp.exp(-x))


def _softplus(x):
    return jnp.maximum(x, 0.0) + jnp.log(1.0 + jnp.exp(-jnp.abs(x)))


def _rms(x, g):
    return x * lax.rsqrt(jnp.mean(x * x, axis=-1, keepdims=True) + RMS_EPS) * g


def _lo_mask(shape):
    return lax.broadcasted_iota(jnp.int32, shape, len(shape) - 1) < HEAD_DIM


def _pair_sum(x, lo):
    s_lo = jnp.sum(jnp.where(lo, x, 0.0), axis=-1, keepdims=True)
    s_hi = jnp.sum(jnp.where(lo, 0.0, x), axis=-1, keepdims=True)
    return jnp.where(lo, s_lo, s_hi)


def _head_rms(x, gain):
    outs = []
    for j in range(x.shape[-1] // LANES):
        xb = x[:, j * LANES:(j + 1) * LANES]
        lo = _lo_mask(xb.shape)
        ms = _pair_sum(xb * xb, lo) * (1.0 / HEAD_DIM)
        outs.append(xb * lax.rsqrt(ms + RMS_EPS))
    y = outs[0] if len(outs) == 1 else jnp.concatenate(outs, axis=-1)
    return y * gain


def _tril(n, strict):
    r = lax.broadcasted_iota(jnp.int32, (n, n), 0)
    c = lax.broadcasted_iota(jnp.int32, (n, n), 1)
    return (r > c) if strict else (r >= c)


def _norm_proj_kernel(x_ref, g_ref, w_ref, *o_refs, splits):
    h = _rms(x_ref[...], g_ref[...]).astype(BF16)
    c0 = 0
    for o_ref, width in zip(o_refs, splits):
        o_ref[...] = jnp.dot(h, w_ref[:, c0:c0 + width],
                             preferred_element_type=F32).astype(o_ref.dtype)
        c0 += width


def _norm_proj(x, g, w, splits, dtypes):
    n, d = x.shape
    tm = ROW_BLOCK
    return pl.pallas_call(
        functools.partial(_norm_proj_kernel, splits=splits),
        out_shape=[jax.ShapeDtypeStruct((n, s), dt) for s, dt in zip(splits, dtypes)],
        grid=(n // tm,),
        in_specs=[pl.BlockSpec((tm, d), lambda i: (i, 0)),
                  _const_spec((1, d)),
                  _const_spec(w.shape)],
        out_specs=[pl.BlockSpec((tm, s), lambda i: (i, 0)) for s in splits],
        compiler_params=_params("parallel"),
        name="norm_proj",
    )(x, g, w)


def _mem_kv_kernel(mem_ref, g_ref, w_ref, kg_ref, k_ref, v_ref):
    h = _rms(mem_ref[0], g_ref[0]).astype(BF16)
    kv = jnp.dot(h, w_ref[0], preferred_element_type=F32)
    k_ref[0, 0] = _head_rms(kv[:, :D_MEMH], kg_ref[0]).astype(BF16)
    v_ref[0, 0] = kv[:, D_MEMH:].astype(BF16)


def _mem_kv(mem, mem_norm, w_mem_kv, mem_k_gain):
    b, m, d = mem.shape
    depth = w_mem_kv.shape[0]
    out = jax.ShapeDtypeStruct((depth, b, m, D_MEMH), BF16)
    return pl.pallas_call(
        _mem_kv_kernel,
        out_shape=[out, out],
        grid=(depth, b),
        in_specs=[pl.BlockSpec((1, m, d), lambda l, i: (i, 0, 0)),
                  pl.BlockSpec((1, 1, d), lambda l, i: (l, 0, 0)),
                  pl.BlockSpec((1, d, 2 * D_MEMH), lambda l, i: (l, 0, 0)),
                  pl.BlockSpec((1, 1, D_MEMH), lambda l, i: (l, 0, 0))],
        out_specs=[pl.BlockSpec((1, 1, m, D_MEMH), lambda l, i: (l, i, 0, 0))] * 2,
        compiler_params=_params("parallel", "parallel"),
        name="mem_kv",
    )(mem, mem_norm[:, None, :], w_mem_kv.astype(BF16),
      jnp.tile(mem_k_gain, (1, N_MEM_HEADS))[:, None, :])


def _mem_attention(q, k, v, q_gain):
    qn = _head_rms(q, q_gain) * ATTN_SCALE
    lane = lax.broadcasted_iota(jnp.int32, qn.shape, 1)
    out = jnp.zeros(qn.shape, F32)
    for h in range(N_MEM_HEADS):
        sel = jnp.logical_and(lane >= h * HEAD_DIM, lane < (h + 1) * HEAD_DIM)
        s = _dot_nt(jnp.where(sel, qn, 0.0), k)
        e = jnp.exp(s - jnp.max(s, axis=-1, keepdims=True))
        o = _dot(e, v) / jnp.sum(e, axis=-1, keepdims=True)
        out = jnp.where(sel, o, out)
    return out


def _rwkv_kernel(u_ref, mu_ref, wup_ref, aup_ref, gup_ref, w0_ref, a0_ref, kk_ref, ka_ref,
                 rk_ref, lng_ref, lnb_ref, o_ref, state_ref, prev_ref):
    L = RWKV_CHUNK
    L2 = 2 * L
    assert L2 == LANES
    n_chunks = u_ref.shape[0] // L
    c = pl.program_id(1)

    @pl.when(c == 0)
    def _():
        state_ref[...] = jnp.zeros_like(state_ref)
        prev_ref[...] = jnp.zeros_like(prev_ref)

    own = (lax.broadcasted_iota(jnp.int32, (L2, LANES), 0) < L) == _lo_mask((L2, LANES))
    tok_r = lax.broadcasted_iota(jnp.int32, (L2, L2), 0) & (L - 1)
    tok_c = lax.broadcasted_iota(jnp.int32, (L2, L2), 1) & (L - 1)
    strict = tok_r > tok_c
    incl = tok_r >= tok_c
    eye = (lax.broadcasted_iota(jnp.int32, (L2, L2), 0)
           == lax.broadcasted_iota(jnp.int32, (L2, L2), 1)).astype(F32)
    pairs = range(N_PAIRS)
    sls = [slice(p * LANES, (p + 1) * LANES) for p in pairs]

    def stack(x):
        return jnp.where(own, jnp.concatenate([x, x], axis=0), 0.0)

    def front(chunks):
        rows = len(chunks) * L
        u = u_ref[...]
        row = lax.broadcasted_iota(jnp.int32, u.shape, 0)
        us = u + (jnp.where(row == 0, prev_ref[...], pltpu.roll(u, 1, axis=0)) - u) * mu_ref[...]

        r = us[:, 0:D_MAIN]
        k = us[:, D_MAIN:2 * D_MAIN]
        v = us[:, 2 * D_MAIN:3 * D_MAIN]
        x_wa = us[:, 3 * D_MAIN:3 * D_MAIN + D_LORA_WA]
        x_g = us[:, 3 * D_MAIN + D_LORA_WA:]

        w_log = -_softplus(-(w0_ref[...] + _dot(jnp.tanh(x_wa), wup_ref[...]))) - 0.5
        log_w = jnp.exp(w_log) * (-LOG2E)
        lr = _sigmoid(a0_ref[...] + _dot(x_wa, aup_ref[...]))
        gate = _dot(_sigmoid(x_g), gup_ref[...])
        kk_raw = k * kk_ref[...]
        k = k * (1.0 + (lr - 1.0) * ka_ref[...])

        tok_i = lax.broadcasted_iota(jnp.int32, (rows, rows), 0)
        tok_j = lax.broadcasted_iota(jnp.int32, (rows, rows), 1)
        shift = L.bit_length() - 1
        same_chunk_tril = jnp.logical_and(tok_i >= tok_j, (tok_i >> shift) == (tok_j >> shift))
        cum = _dot_f32(same_chunk_tril.astype(F32), log_w)
        g_incl = jnp.exp2(cum)
        g_excl = jnp.exp2(cum - log_w)
        g_inv = jnp.exp2(-cum)
        g_end = jnp.concatenate(
            [jnp.broadcast_to(g_incl[(i + 1) * L - 1:(i + 1) * L, :], (L, D_MAIN))
             for i in range(len(chunks))], axis=0)
        g_tail = g_end * g_inv

        lo = _lo_mask((rows, LANES))
        out = {"r": r, "k": k, "v": v, "gate": gate, "lo": lo}
        for p in pairs:
            sl = sls[p]
            kk = kk_raw[:, sl]
            kkn = kk * lax.rsqrt(jnp.maximum(_pair_sum(kk * kk, lo), KK_NORM_FLOOR))
            b = kkn * lr[:, sl]
            a_dec = -kkn * g_excl[:, sl]
            r_dec = r[:, sl] * g_incl[:, sl]
            b_inv = b * g_inv[:, sl]
            k_inv = k[:, sl] * g_inv[:, sl]
            b_tail = b * g_tail[:, sl]
            k_tail = k[:, sl] * g_tail[:, sl]
            for i, ci in enumerate(chunks):
                rs = slice(i * L, (i + 1) * L)
                ar = jnp.concatenate([stack(a_dec[rs]), stack(r_dec[rs])],
                                     axis=0).astype(BF16)
                v2 = stack(v[rs, sl]).astype(BF16)
                bk_inv = jnp.concatenate([stack(b_inv[rs]), stack(k_inv[rs])], axis=0)
                g = _dot_nt(ar, bk_inv)
                a_r = jnp.concatenate([jnp.where(incl, g[L2:, :L2], 0.0),
                                       jnp.where(incl, g[L2:, L2:], 0.0)],
                                      axis=1).astype(BF16)
                out[ci, p] = {
                    "ar": ar, "v2": v2, "a_r": a_r,
                    "power": jnp.where(strict, g[:L2, :L2], 0.0),
                    "a_ak": jnp.where(strict, g[:L2, L2:], 0.0).astype(BF16),
                    "bk_tail": jnp.concatenate([stack(b_tail[rs]), stack(k_tail[rs])], axis=0),
                    "decay": g_incl[(i + 1) * L - 1:(i + 1) * L, sl]}
        return out

    def invert(fr, chunks):
        units = [(ci, p) for ci in chunks for p in pairs]
        power = {un: fr[un]["power"] for un in units}
        inv = {un: eye + power[un] for un in units}
        for _ in range(L.bit_length() - 2):
            power = {un: _dot(power[un], power[un]) for un in units}
            inv = {un: inv[un] + _dot(power[un], inv[un]) for un in units}
        for un in units:
            fr[un]["inv"] = inv[un]
            fr[un]["from_v"] = _dot(fr[un]["a_ak"], fr[un]["v2"])

    def advance(fr, chunks, states):
        ys = {p: [] for p in pairs}
        for ci in chunks:
            un = [fr[ci, p] for p in pairs]
            from_state = [_dot_nt(un[p]["ar"], states[p]) for p in pairs]
            uv = [jnp.concatenate([_dot(un[p]["inv"], from_state[p][:L2] + un[p]["from_v"]
                                        ).astype(BF16), un[p]["v2"]], axis=0) for p in pairs]
            for p in pairs:
                y2 = from_state[p][L2:] + _dot(un[p]["a_r"], uv[p])
                ys[p].append(y2[:L] + y2[L:])
            states = [states[p] * un[p]["decay"] + _dot_tn(uv[p], un[p]["bk_tail"]) for p in pairs]
        return ys, states

    def finish(fr, ys):
        lo = fr["lo"]
        for p in pairs:
            sl = sls[p]
            y = jnp.concatenate(ys[p], axis=0)
            mean = _pair_sum(y, lo) * (1.0 / HEAD_DIM)
            yc = y - mean
            var = _pair_sum(yc * yc, lo) * (1.0 / HEAD_DIM)
            yn = yc * lax.rsqrt(var + GN_EPS) * lng_ref[:, sl] + lnb_ref[:, sl]
            bonus = _pair_sum(fr["r"][:, sl] * fr["k"][:, sl] * rk_ref[:, sl], lo) * fr["v"][:, sl]
            o_ref[:, sl] = ((yn + bonus) * fr["gate"][:, sl]).astype(o_ref.dtype)

    chunks = list(range(n_chunks))
    last_row = u_ref[n_chunks * L - 1:n_chunks * L, :]
    states = [state_ref[p] for p in pairs]
    prepared = front(chunks)
    invert(prepared, chunks)
    ys, states = advance(prepared, chunks, states)
    finish(prepared, ys)
    prev_ref[...] = last_row
    for p in pairs:
        state_ref[p] = states[p]


def _rwkv(u_main, bsz, mu, w_up, a_up, g_up, w0, a0, k_k, k_a, r_k, lnx_g, lnx_b):
    n, width = u_main.shape
    step_rows = RWKV_CHUNK * RWKV_STEP_CHUNKS
    nc = n // bsz // step_rows
    rows = [w0, a0, k_k, k_a, r_k.reshape(-1), lnx_g, lnx_b]
    zeros_wa = jnp.zeros((D_DECAY_LORA, D_MAIN), F32)
    wup = jnp.concatenate([w_up, zeros_wa], axis=0)
    aup = jnp.concatenate([zeros_wa, a_up], axis=0)
    gup = jnp.concatenate([g_up, jnp.zeros((D_LORA_G - D_GATE_LORA, D_MAIN), F32)], axis=0)
    return pl.pallas_call(
        _rwkv_kernel,
        out_shape=jax.ShapeDtypeStruct((n, D_MAIN), BF16),
        grid=(bsz, nc),
        in_specs=[pl.BlockSpec((step_rows, width), lambda b, c: (b * nc + c, 0)),
                  _const_spec((1, width)),
                  _const_spec(wup.shape), _const_spec(aup.shape), _const_spec(gup.shape)]
                 + [_const_spec((1, D_MAIN))] * len(rows),
        out_specs=pl.BlockSpec((step_rows, D_MAIN), lambda b, c: (b * nc + c, 0)),
        scratch_shapes=[pltpu.VMEM((N_PAIRS, LANES, LANES), F32),
                        pltpu.VMEM((1, width), F32)],
        compiler_params=_params("parallel", "arbitrary"),
        name="rwkv7",
    )(u_main, mu[None, :], wup, aup, gup, *[x[None, :] for x in rows])


def _split3(x):
    hi = x.astype(BF16).astype(F32)
    mid = (x - hi).astype(BF16).astype(F32)
    lo = (x - hi - mid).astype(BF16).astype(F32)
    return hi, mid, lo


def _key_bias(c2, base):
    lane = lax.broadcasted_iota(jnp.int32, c2.shape, 1)
    terms = _split3(jnp.where(lane < N_MAIN_HEADS, -c2, 0.0))
    out = jnp.where(jnp.logical_and(lane >= base + 3 * N_MAIN_HEADS,
                                    lane < base + 3 * N_MAIN_HEADS + 3), 1.0, 0.0)
    for i, term in enumerate(terms):
        shift = base + i * N_MAIN_HEADS
        out = out + (pltpu.roll(term, shift, axis=1) if shift else term)
    return out


def _query_bias(lane, base, head, cq):
    out = jnp.zeros(lane.shape, F32)
    for i in range(3):
        out = jnp.where(lane == base + i * N_MAIN_HEADS + head, 1.0, out)
    for i, term in enumerate(_split3(cq)):
        out = jnp.where(lane == base + 3 * N_MAIN_HEADS + i, term, out)
    return out


def _shared_kv_kernel(x_ref, g_ref, w_ref, wvt_ref, bf_ref, kg_ref, k_ref, vt_ref, c_ref,
                      carry_ref):
    @pl.when(pl.program_id(1) == 0)
    def _():
        carry_ref[...] = jnp.zeros_like(carry_ref)

    tm = x_ref.shape[0]
    h = _rms(x_ref[...], g_ref[...]).astype(BF16)
    vt_ref[0] = _dot_nt(wvt_ref[...], h).astype(BF16)
    logits = jnp.dot(h, w_ref[:, D_MAIN:], preferred_element_type=F32) + bf_ref[...]
    log_f = -_softplus(-logits)
    tri = _tril(CUMSUM_BLOCK, strict=False).astype(F32)
    carry = carry_ref[...]
    for j in range(0, tm, CUMSUM_BLOCK):
        cj = _dot_f32(tri, log_f[j:j + CUMSUM_BLOCK]) + carry
        c_ref[j:j + CUMSUM_BLOCK, :] = cj * LOG2E
        carry = cj[CUMSUM_BLOCK - 1:CUMSUM_BLOCK, :]
    carry_ref[...] = carry

    c2 = c_ref[...]
    lo = _lo_mask((tm, LANES))
    bias = (_key_bias(c2, HEAD_DIM), _key_bias(c2, 0))
    for p in range(N_PAIRS):
        sl = slice(p * LANES, (p + 1) * LANES)
        kn = _head_rms(jnp.dot(h, w_ref[:, sl], preferred_element_type=F32), kg_ref[:, sl])
        for j in range(2):
            head = 2 * p + j
            own = lo if j == 0 else jnp.logical_not(lo)
            k_ref[:, head * LANES:(head + 1) * LANES] = jnp.where(own, kn, bias[j]).astype(BF16)


def _shared_kv(x, bsz, kv_norm, w_kv, b_f, k_gain):
    n, d = x.shape
    tm = ROW_BLOCK
    nt = n // bsz // tm
    pad = LANES - N_MAIN_HEADS
    w = jnp.concatenate([w_kv[:, :D_MAIN], jnp.pad(w_kv[:, 2 * D_MAIN:], ((0, 0), (0, pad)))],
                        axis=1).astype(BF16)
    w_vt = w_kv[:, D_MAIN:2 * D_MAIN].T.astype(BF16)
    row = lambda b, i: (b * nt + i, 0)
    return pl.pallas_call(
        _shared_kv_kernel,
        out_shape=[jax.ShapeDtypeStruct((n, N_MAIN_HEADS * LANES), BF16),
                   jax.ShapeDtypeStruct((bsz, D_MAIN, n // bsz), BF16),
                   jax.ShapeDtypeStruct((n, LANES), F32)],
        grid=(bsz, nt),
        in_specs=[pl.BlockSpec((tm, d), row), _const_spec((1, d)), _const_spec(w.shape),
                  _const_spec(w_vt.shape), _const_spec((1, LANES)), _const_spec((1, D_MAIN))],
        out_specs=[pl.BlockSpec((tm, N_MAIN_HEADS * LANES), row),
                   pl.BlockSpec((1, D_MAIN, tm), lambda b, i: (b, 0, i)),
                   pl.BlockSpec((tm, LANES), row)],
        scratch_shapes=[pltpu.VMEM((1, LANES), F32)],
        compiler_params=_params("parallel", "arbitrary"),
        name="shared_kv",
    )(x, kv_norm[None, :], w, w_vt, jnp.pad(b_f, (0, pad))[None, :],
      jnp.tile(k_gain, N_MAIN_HEADS)[None, :])


def _fox_kernel(q_ref, k0_ref, k1_ref, vt_ref, c_ref, cend_ref, gate_ref, qg_ref, thr_ref, o_ref,
                sa_ref, sb_ref, p_ref, acc_ref):
    tq = q_ref.shape[1]
    p = pl.program_id(1)
    qi = pl.program_id(2)
    c_ends = cend_ref[0]
    head_lane = lax.broadcasted_iota(jnp.int32, c_ends.shape, 1) >> 1
    keep = jnp.where(jnp.logical_and(head_lane == p, c_ref[0, 0:1, :] - c_ends >= thr_ref[...]),
                     1.0, 0.0)
    skippable = jnp.sum(keep, axis=1, keepdims=True) == 0.0
    first = jnp.sum(jnp.where(skippable, 1, 0)).astype(jnp.int32)
    qn = _head_rms(q_ref[0], qg_ref[...]) * (ATTN_SCALE * LOG2E)
    lane = lax.broadcasted_iota(jnp.int32, qn.shape, 1)
    lo = lane < HEAD_DIM
    c2 = c_ref[0]
    k_refs = (k0_ref, k1_ref)
    q_tiles = []
    for j in range(2):
        own = lo if j == 0 else jnp.logical_not(lo)
        cq = jnp.sum(jnp.where(lane == 2 * p + j, c2, 0.0), axis=-1, keepdims=True)
        bias = _query_bias(lane, HEAD_DIM * (1 - j), 2 * p + j, cq)
        q_tiles.append(jnp.where(own, qn, bias).astype(BF16))
    acc_ref[...] = jnp.zeros_like(acc_ref)
    tk = sa_ref.shape[1]
    key_idx = lax.broadcasted_iota(jnp.int32, (tk, LANES), 0)
    qry_idx = lax.broadcasted_iota(jnp.int32, (tk, LANES), 1)

    def scores(block, s_ref, q_lo=0):
        start = pl.multiple_of(block * tk, tk)
        for j in range(2):
            s_ref[j, :, q_lo:] = _dot_nt(k_refs[j][0, pl.ds(start, tk), :],
                                         q_tiles[j][q_lo:, :])

    def step(block, s_ref, slot, carry, key_offset, q_lo=0):
        keys = pl.ds(pl.multiple_of(block * tk, tk), tk)
        ones = jnp.ones((SUM_ROWS, tk), BF16)
        out = []
        for j in range(2):
            vtb = jnp.concatenate([vt_ref[0, j * HEAD_DIM:(j + 1) * HEAD_DIM, keys], ones], axis=0)
            m_old = carry[j]
            m_parts, a_parts = [m_old[:, :q_lo]] if q_lo else [], []
            for g in range(q_lo, tq, LANES):
                gs = slice(g, g + LANES)
                s = s_ref[j, :, gs]
                if key_offset is not None:
                    s = jnp.where(key_idx + key_offset <= qry_idx + g, s, NEG_INF)
                m_new = jnp.maximum(m_old[:, gs], jnp.max(s, axis=0, keepdims=True))
                m_parts.append(m_new)
                a_parts.append(jnp.exp2(m_old[:, gs] - m_new))
                p_ref[slot, j, :, gs] = jnp.exp2((s - m_new).astype(BF16))
            alpha = jnp.concatenate(a_parts, axis=1)
            acc_ref[j, :, q_lo:] = acc_ref[j, :, q_lo:] * alpha + jnp.dot(
                vtb, p_ref[slot, j, :, q_lo:], preferred_element_type=F32)
            out.append(jnp.concatenate(m_parts, axis=1))
        return tuple(out)

    init = jnp.full((1, tq), NEG_INF, F32)

    def body(t, carry):
        scores(2 * t, sa_ref)
        scores(2 * t + 1, sb_ref)
        carry = step(2 * t, sa_ref, 0, carry, None)
        return step(2 * t + 1, sb_ref, 1, carry, None)

    def sweep(n_back):
        half_steps = [(2 * qi, 0, 0), (2 * qi + 1, tk, tk)]
        for back in range(1, n_back + 1):
            half_steps += [(2 * (qi - back), None, 0), (2 * (qi - back) + 1, None, 0)]
        bufs = (sa_ref, sb_ref)
        scores(half_steps[0][0], sa_ref)
        scores(half_steps[1][0], sb_ref, tk)
        carry = (init, init)
        for i, (block, key_offset, q_lo) in enumerate(half_steps):
            carry = step(block, bufs[i % 2], i % 2, carry, key_offset, q_lo)
            if i + 2 < len(half_steps):
                scores(half_steps[i + 2][0], bufs[i % 2])
        return carry

    @pl.when(qi == 0)
    def _():
        sweep(0)

    @pl.when(qi == 1)
    def _():
        sweep(1)

    @pl.when(qi >= 2)
    def _():
        lax.fori_loop(first, qi - 2, body, sweep(2))
    o = jnp.concatenate([acc_ref[j, :HEAD_DIM, :] / acc_ref[j, HEAD_DIM:HEAD_DIM + 1, :]
                         for j in range(2)], axis=0).T
    o_ref[0] = (o * _sigmoid(gate_ref[0])).astype(o_ref.dtype)


def _fox(q, gate, k_tiles, v_t, c2, q_gain, k_gain, bsz):
    n, _ = q.shape
    t = n // bsz
    tq = ATTN_BLOCK
    c2 = c2.reshape(bsz, t, LANES)
    c_ends = c2.reshape(bsz, t // tq, tq, LANES)[:, :, tq - 1, :]
    qk_bound = ((1.01 * HEAD_DIM * ATTN_SCALE * LOG2E)
                * jnp.max(jnp.abs(q_gain)) * jnp.max(jnp.abs(k_gain)))
    thr = jnp.full((1, LANES), -(2.0 * qk_bound + UNDERFLOW_LOG2), F32)
    blk = pl.BlockSpec((1, tq, LANES), lambda b, p, i: (b, i, p))
    out = pl.pallas_call(
        _fox_kernel,
        out_shape=jax.ShapeDtypeStruct((bsz, t, D_MAIN), BF16),
        grid=(bsz, N_PAIRS, t // tq),
        in_specs=[blk,
                  pl.BlockSpec((1, t, LANES), lambda b, p, i: (b, 0, 2 * p)),
                  pl.BlockSpec((1, t, LANES), lambda b, p, i: (b, 0, 2 * p + 1)),
                  pl.BlockSpec((1, LANES, t), lambda b, p, i: (b, p, 0)),
                  pl.BlockSpec((1, tq, LANES), lambda b, p, i: (b, i, 0)),
                  pl.BlockSpec((1, t // tq, LANES), lambda b, p, i: (b, 0, 0)),
                  blk, _const_spec((1, LANES)), _const_spec((1, LANES))],
        out_specs=blk,
        scratch_shapes=[pltpu.VMEM((2, tq // 2, tq), F32), pltpu.VMEM((2, tq // 2, tq), F32),
                        pltpu.VMEM((2, 2, tq // 2, tq), BF16),
                        pltpu.VMEM((2, HEAD_DIM + SUM_ROWS, tq), F32)],
        compiler_params=_params("parallel", "parallel", "arbitrary"),
        name="fox_attn",
    )(q.reshape(bsz, t, D_MAIN), k_tiles.reshape(bsz, t, N_MAIN_HEADS * LANES),
      k_tiles.reshape(bsz, t, N_MAIN_HEADS * LANES), v_t,
      c2, c_ends, gate.reshape(bsz, t, D_MAIN), jnp.tile(q_gain, 2)[None, :], thr)
    return out.reshape(n, D_MAIN)


def _merge_ffn_kernel(x_ref, main_ref, mq_ref, mk_ref, mv_ref, mqg_ref, wo_ref, g_ref, wgu_ref,
                      wd_ref, o_ref):
    d_ff = wd_ref.shape[0]
    memo = _mem_attention(mq_ref[...], mk_ref[0, 0], mv_ref[0, 0], mqg_ref[...]).astype(BF16)
    mixed = (jnp.dot(main_ref[...], wo_ref[:D_MAIN, :], preferred_element_type=F32)
             + jnp.dot(memo, wo_ref[D_MAIN:, :], preferred_element_type=F32))
    x = x_ref[...] + mixed
    h = _rms(x, g_ref[...]).astype(BF16)
    gate = jnp.dot(h, wgu_ref[:, :d_ff], preferred_element_type=F32)
    up = jnp.dot(h, wgu_ref[:, d_ff:], preferred_element_type=F32)
    act = (gate * _sigmoid(gate) * up).astype(BF16)
    o_ref[...] = x + jnp.dot(act, wd_ref[...], preferred_element_type=F32)


def _merge_ffn(x, main, mem_q, k_mem, v_mem, mem_q_gain, layer, bsz, w_out, ffn_norm, w_gate_up,
               w_down):
    n, d = x.shape
    tm = ROW_BLOCK
    blocks_per_batch = n // bsz // tm
    row = lambda i: (i, 0)
    single = pl.Buffered(1)
    mem_spec = pl.BlockSpec((1, 1) + k_mem.shape[2:], lambda i: (layer, i // blocks_per_batch, 0, 0))
    return pl.pallas_call(
        _merge_ffn_kernel,
        out_shape=jax.ShapeDtypeStruct((n, d), F32),
        grid=(n // tm,),
        in_specs=[pl.BlockSpec((tm, d), row), pl.BlockSpec((tm, D_MAIN), row),
                  pl.BlockSpec((tm, D_MEMH), row), mem_spec, mem_spec, _const_spec((1, D_MEMH)),
                  pl.BlockSpec(w_out.shape, lambda i: (0, 0), pipeline_mode=single),
                  _const_spec((1, d)),
                  pl.BlockSpec(w_gate_up.shape, lambda i: (0, 0), pipeline_mode=single),
                  pl.BlockSpec(w_down.shape, lambda i: (0, 0), pipeline_mode=single)],
        out_specs=pl.BlockSpec((tm, d), row),
        compiler_params=_params("parallel"),
        name="merge_ffn",
    )(x, main, mem_q, k_mem, v_mem, jnp.tile(mem_q_gain, N_MEM_HEADS)[None, :],
      w_out.astype(BF16), ffn_norm[None, :], w_gate_up.astype(BF16), w_down.astype(BF16))


def _a_in_weights(w_in, mu):
    d_shift = 3 * D_MAIN + D_DECAY_LORA + D_AAA_LORA + D_GATE_LORA
    pad = D_LORA_G - D_GATE_LORA
    w = jnp.concatenate([w_in[:, :d_shift], jnp.zeros((w_in.shape[0], pad), w_in.dtype),
                         w_in[:, d_shift:]], axis=1)
    return w.astype(BF16), jnp.pad(mu, (0, pad))


def kernel(x, mem, mix_norm, w_out, mem_norm, w_mem_kv, mem_q_gain, mem_k_gain, ffn_norm, w_gate_up, w_down, a_w_in, a_mu, a_w0, a_w_up, a_a0, a_a_up, a_g_up, a_k_k, a_k_a, a_r_k, a_lnx_g, a_lnx_b, kv_norm, w_kv, b_f, k_gain, b_w_in, b_q_gain):
    bsz, t, d = x.shape
    n_a = a_w_in.shape[0]
    n_b = b_w_in.shape[0]
    x = x.reshape(bsz * t, d)
    k_mem, v_mem = _mem_kv(mem, mem_norm, w_mem_kv, mem_k_gain)

    def merge(x, l, main, mem_q):
        return _merge_ffn(x, main, mem_q, k_mem, v_mem, mem_q_gain[l], l, bsz, w_out[l],
                          ffn_norm[l], w_gate_up[l], w_down[l])

    for i in range(n_a):
        w, mu = _a_in_weights(a_w_in[i], a_mu[i])
        u_main, mem_q = _norm_proj(x, mix_norm[i][None, :], w, (D_A_MAIN, D_MEMH), (F32, F32))
        main = _rwkv(u_main, bsz, mu, a_w_up[i], a_a_up[i], a_g_up[i], a_w0[i], a_a0[i],
                     a_k_k[i], a_k_a[i], a_r_k[i], a_lnx_g[i], a_lnx_b[i])
        x = merge(x, i, main, mem_q)

    k_sh, v_sh, c_sh = _shared_kv(x, bsz, kv_norm, w_kv, b_f, k_gain)

    for j in range(n_b):
        l = n_a + j
        q, gate, mem_q = _norm_proj(x, mix_norm[l][None, :], b_w_in[j].astype(BF16),
                                    (D_MAIN, D_MAIN, D_MEMH), (F32, F32, F32))
        main = _fox(q, gate, k_sh, v_sh, c_sh, b_q_gain[j], k_gain, bsz)
        x = merge(x, l, main, mem_q)

    return x.reshape(bsz, t, d)
```
